```python
import math
import jax, jax.numpy as jnp
from jax import lax
import numpy as np

D_MODEL = 1024
BATCH = 2
SEQ = 16384
DEPTH = 4

HEAD_DIM = 64
N_MIX_HEADS = 8
N_MEM_HEADS = 4
MIX_WIDTH = N_MIX_HEADS * HEAD_DIM
MEM_WIDTH = N_MEM_HEADS * HEAD_DIM
MERGED_WIDTH = MIX_WIDTH + MEM_WIDTH
N_MEM = 256
D_FF = 4 * D_MODEL
BLOCK_Q = 128
GROUP_Q = 1024
N_A = DEPTH // 2
N_B = DEPTH - N_A
W_IN_A = 3 * MIX_WIDTH + MEM_WIDTH
W_IN_B = MIX_WIDTH + MEM_WIDTH
W_KV_SHARED = 2 * MIX_WIDTH + N_MIX_HEADS
EPS = 1e-6
NEG_INF = -1e30
FORGET_BIAS_INIT = 2.0

kernel_name = "yoco_stickbreak_fox_hybrid"


def rms_norm(x, g):
    xf = x.astype(jnp.float32)
    y = xf * lax.rsqrt(jnp.mean(xf * xf, axis=-1, keepdims=True) + EPS) * g.astype(jnp.float32)
    return y.astype(x.dtype)


def split_heads(t, n_heads):
    b, s, _ = t.shape
    return t.reshape(b, s, n_heads, HEAD_DIM).transpose(0, 2, 1, 3)


def merge_heads(t):
    b, h, s, d = t.shape
    return t.transpose(0, 2, 1, 3).reshape(b, s, h * d)


def to_blocks(t):
    b, h, s = t.shape[:3]
    t = t.reshape((b, h, s // BLOCK_Q, BLOCK_Q) + t.shape[3:])
    return jnp.moveaxis(t, 2, 0)


def from_blocks(t):
    nb, b, h, blk, d = t.shape
    return jnp.moveaxis(t, 0, 2).reshape(b, h, nb * blk, d)


def causal_sweep(block_fn, q_side, kv_side):
    s_len = q_side[0].shape[2]
    outs = []
    for g0 in range(0, s_len, GROUP_Q):
        g1 = min(g0 + GROUP_Q, s_len)
        kv_g = tuple(t[:, :, :g1] for t in kv_side)
        q_blocks = tuple(to_blocks(t[:, :, g0:g1]) for t in q_side)
        starts = g0 + jnp.arange((g1 - g0) // BLOCK_Q) * BLOCK_Q

        def body(args, kv_g=kv_g):
            return block_fn(*args, *kv_g)

        outs.append(from_blocks(lax.map(body, q_blocks + (starts,))))
    return jnp.concatenate(outs, axis=2)


def _stick_breaking_block(q_blk, t0, k, v):
    scale = 1.0 / math.sqrt(HEAD_DIM)
    z = jnp.einsum('bhqd,bhkd->bhqk', q_blk, k).astype(jnp.float32) * scale
    key_pos = jnp.arange(k.shape[2])
    t_pos = t0 + jnp.arange(BLOCK_Q)
    mask = key_pos[None, :] < t_pos[:, None]
    log_one_minus = jnp.where(mask, jax.nn.log_sigmoid(-z), 0.0)
    between = lax.cumsum(log_one_minus, axis=3, reverse=True) - log_one_minus
    w = jnp.where(mask, jnp.exp(jax.nn.log_sigmoid(z) + between), 0.0)
    return jnp.einsum('bhqk,bhkd->bhqd', w.astype(v.dtype), v)


def stick_breaking_attention(q, k, v):
    return causal_sweep(_stick_breaking_block, (q,), (k, v))


def _forgetting_block(q_blk, c_blk, t0, k, v, c_k):
    scale = 1.0 / math.sqrt(HEAD_DIM)
    z = jnp.einsum('bhqd,bhkd->bhqk', q_blk, k).astype(jnp.float32) * scale
    z = z + c_blk[..., :, None] - c_k[:, :, None, :]
    key_pos = jnp.arange(k.shape[2])
    t_pos = t0 + jnp.arange(BLOCK_Q)
    mask = key_pos[None, :] <= t_pos[:, None]
    p = jax.nn.softmax(jnp.where(mask, z, NEG_INF), axis=-1)
    return jnp.einsum('bhqk,bhkd->bhqd', p.astype(v.dtype), v)


def forgetting_attention(q, k, v, log_f_cum):
    return causal_sweep(_forgetting_block, (q, log_f_cum), (k, v, log_f_cum))


def memory_attention(q_mem, mem_k, mem_v):
    scale = 1.0 / math.sqrt(HEAD_DIM)
    s = jnp.einsum('bshd,bmhd->bhsm', q_mem, mem_k).astype(jnp.float32) * scale
    p = jax.nn.softmax(s, axis=-1)
    o = jnp.einsum('bhsm,bmhd->bshd', p.astype(mem_v.dtype), mem_v)
    b, sl = q_mem.shape[:2]
    return o.reshape(b, sl, MEM_WIDTH)


def squared_relu_mlp(x, w1, w2):
    h = jnp.square(jax.nn.relu(x @ w1))
    return h @ w2


def setup_inputs(seed: int = 0) -> dict:
    key = jax.random.key(seed)
    ks = jax.random.split(key, 16)
    f32 = jnp.float32
    nrm = lambda k, shape, s: (jax.random.normal(k, shape, f32) * s).astype(f32)
    x = jax.random.normal(ks[0], (BATCH, SEQ, D_MODEL), f32)
    mem = jax.random.normal(ks[1], (BATCH, N_MEM, D_MODEL), f32)
    norm1_g = 1.0 + nrm(ks[2], (DEPTH, D_MODEL), 0.02)
    w_in_a = nrm(ks[3], (N_A, D_MODEL, W_IN_A), D_MODEL ** -0.5)
    w_in_b = nrm(ks[4], (N_B, D_MODEL, W_IN_B), D_MODEL ** -0.5)
    w_mem_kv = nrm(ks[5], (DEPTH, D_MODEL, 2 * MEM_WIDTH), D_MODEL ** -0.5)
    mem_norm_g = 1.0 + nrm(ks[6], (DEPTH, D_MODEL), 0.02)
    w_o = nrm(ks[7], (DEPTH, MERGED_WIDTH, D_MODEL), MERGED_WIDTH ** -0.5)
    norm2_g = 1.0 + nrm(ks[8], (DEPTH, D_MODEL), 0.02)
    w_mlp1 = nrm(ks[9], (DEPTH, D_MODEL, D_FF), D_MODEL ** -0.5)
    w_mlp2 = nrm(ks[10], (DEPTH, D_FF, D_MODEL), D_FF ** -0.5)
    kv_norm_g = 1.0 + nrm(ks[11], (D_MODEL,), 0.02)
    w_kv_shared = nrm(ks[12], (D_MODEL, W_KV_SHARED), D_MODEL ** -0.5)
    b_f = FORGET_BIAS_INIT + nrm(ks[13], (N_MIX_HEADS,), 0.1)
    final_norm_g = 1.0 + nrm(ks[14], (D_MODEL,), 0.02)
    return {"x": x, "mem": mem, "norm1_g": norm1_g, "w_in_a": w_in_a, "w_in_b": w_in_b,
            "w_mem_kv": w_mem_kv, "mem_norm_g": mem_norm_g, "w_o": w_o, "norm2_g": norm2_g,
            "w_mlp1": w_mlp1, "w_mlp2": w_mlp2, "kv_norm_g": kv_norm_g,
            "w_kv_shared": w_kv_shared, "b_f": b_f, "final_norm_g": final_norm_g}


def reference(x, mem, norm1_g, w_in_a, w_in_b, w_mem_kv, mem_norm_g, w_o, norm2_g,
              w_mlp1, w_mlp2, kv_norm_g, w_kv_shared, b_f, final_norm_g):
    b, s_len, _ = x.shape
    m_len = mem.shape[1]
    h = x
    k_sh = v_sh = log_f_cum = None
    for l in range(DEPTH):
        if l == N_A:
            hs = rms_norm(h, kv_norm_g)
            kvf = hs @ w_kv_shared
            k_sh = split_heads(kvf[..., :MIX_WIDTH], N_MIX_HEADS)
            v_sh = split_heads(kvf[..., MIX_WIDTH:2 * MIX_WIDTH], N_MIX_HEADS)
            f_logit = kvf[..., 2 * MIX_WIDTH:].astype(jnp.float32) + b_f.astype(jnp.float32)
            log_f_cum = jnp.moveaxis(lax.cumsum(jax.nn.log_sigmoid(f_logit), axis=1), 1, 2)

        hn = rms_norm(h, norm1_g[l])
        mkv = rms_norm(mem, mem_norm_g[l]) @ w_mem_kv[l]
        mem_k = mkv[..., :MEM_WIDTH].reshape(b, m_len, N_MEM_HEADS, HEAD_DIM)
        mem_v = mkv[..., MEM_WIDTH:].reshape(b, m_len, N_MEM_HEADS, HEAD_DIM)

        if l < N_A:
            proj = hn @ w_in_a[l]
            q = split_heads(proj[..., :MIX_WIDTH], N_MIX_HEADS)
            k = split_heads(proj[..., MIX_WIDTH:2 * MIX_WIDTH], N_MIX_HEADS)
            v = split_heads(proj[..., 2 * MIX_WIDTH:3 * MIX_WIDTH], N_MIX_HEADS)
            q_mem = proj[..., 3 * MIX_WIDTH:]
            mix = stick_breaking_attention(q, k, v)
        else:
            proj = hn @ w_in_b[l - N_A]
            q = split_heads(proj[..., :MIX_WIDTH], N_MIX_HEADS)
            q_mem = proj[..., MIX_WIDTH:]
            mix = forgetting_attention(q, k_sh, v_sh, log_f_cum)

        mem_out = memory_attention(q_mem.reshape(b, s_len, N_MEM_HEADS, HEAD_DIM), mem_k, mem_v)
        merged = jnp.concatenate([merge_heads(mix), mem_out], axis=-1)
        h = h + merged @ w_o[l]
        h = h + squared_relu_mlp(rms_norm(h, norm2_g[l]), w_mlp1[l], w_mlp2[l])
    return rms_norm(h, final_norm_g)
```

```python
import functools

import jax
import jax.numpy as jnp
from jax import lax
from jax.experimental import pallas as pl
from jax.experimental.pallas import tpu as pltpu

D_MODEL = 1024
HEAD_DIM = 64
N_MIX_HEADS = 8
N_MEM_HEADS = 4
MIX_WIDTH = N_MIX_HEADS * HEAD_DIM
MEM_WIDTH = N_MEM_HEADS * HEAD_DIM
D_FF = 4 * D_MODEL
EPS = 1e-6
NEG_INF = -1e30

LANES = 128
HEADS_PER_STEP = LANES // HEAD_DIM
N_PAIRS = N_MIX_HEADS // HEADS_PER_STEP
GATE_LANES = 8
N_PIECES = 3

ROW_TILE = 512
TQ = 256
TK = 256
FF_TILE = 1024
VMEM_LIMIT = 56 * 1024 * 1024

BF16 = jnp.bfloat16
F32 = jnp.float32


def _dot(a, b):
    return jnp.dot(a, b, preferred_element_type=F32)


def _dot_nt(a, b):
    return lax.dot_general(a, b, (((1,), (1,)), ((), ())), preferred_element_type=F32)


def _rms_norm_rows(x, g):
    return x * lax.rsqrt(jnp.mean(x * x, axis=-1, keepdims=True) + EPS) * g


def _const_spec(shape):
    return pl.BlockSpec(shape, lambda *_: (0,) * len(shape))


def _norm_proj_kernel(x_ref, g_ref, w_ref, *o_refs):
    xn = _rms_norm_rows(x_ref[...], g_ref[...]).astype(BF16)
    y = _dot(xn, w_ref[...])
    off = 0
    for o_ref in o_refs:
        width = o_ref.shape[-1]
        o_ref[...] = y[:, off:off + width].astype(o_ref.dtype)
        off += width


def _norm_proj(x, g, w, widths):
    rows = x.shape[0]
    n = w.shape[1]
    assert sum(widths) == n and rows % ROW_TILE == 0
    return pl.pallas_call(
        _norm_proj_kernel,
        grid=(rows // ROW_TILE,),
        in_specs=[
            pl.BlockSpec((ROW_TILE, D_MODEL), lambda i: (i, 0)),
            _const_spec((1, D_MODEL)),
            _const_spec((D_MODEL, n)),
        ],
        out_specs=[pl.BlockSpec((ROW_TILE, wd), lambda i: (i, 0)) for wd in widths],
        out_shape=[jax.ShapeDtypeStruct((rows, wd), BF16) for wd in widths],
        compiler_params=pltpu.CompilerParams(
            dimension_semantics=("arbitrary",), vmem_limit_bytes=VMEM_LIMIT),
    )(x, g.reshape(1, D_MODEL), w)


def _mem_kv_kernel(x_ref, g_ref, w_ref, o_ref):
    xn = _rms_norm_rows(x_ref[...], g_ref[0]).astype(BF16)
    o_ref[0] = _dot(xn, w_ref[0]).astype(o_ref.dtype)


def _mem_kv(mem_rows, gains, w):
    depth = w.shape[0]
    rows = mem_rows.shape[0]
    return pl.pallas_call(
        _mem_kv_kernel,
        grid=(depth,),
        in_specs=[
            _const_spec((rows, D_MODEL)),
            pl.BlockSpec((1, 1, D_MODEL), lambda l: (l, 0, 0)),
            pl.BlockSpec((1, D_MODEL, 2 * MEM_WIDTH), lambda l: (l, 0, 0)),
        ],
        out_specs=pl.BlockSpec((1, rows, 2 * MEM_WIDTH), lambda l: (l, 0, 0)),
        out_shape=jax.ShapeDtypeStruct((depth, rows, 2 * MEM_WIDTH), BF16),
        compiler_params=pltpu.CompilerParams(
            dimension_semantics=("arbitrary",), vmem_limit_bytes=VMEM_LIMIT),
    )(mem_rows, gains.reshape(depth, 1, D_MODEL), w)


def _split3(c):
    hi = c.astype(BF16)
    r1 = c - hi.astype(F32)
    mid = r1.astype(BF16)
    lo = (r1 - mid.astype(F32)).astype(BF16)
    return hi, mid, lo


def _shared_kv_kernel(x_ref, g_ref, w_ref, bf_ref, eq_ref, ek_ref, oneq_ref, onek_ref,
                      k_ref, v_ref, cq_ref, ck_ref, carry_ref, *, tiles_per_seq):
    i = pl.program_id(0)

    @pl.when(i % tiles_per_seq == 0)
    def _():
        carry_ref[...] = jnp.zeros_like(carry_ref)

    xn = _rms_norm_rows(x_ref[...], g_ref[...]).astype(BF16)
    y = _dot(xn, w_ref[...])
    k_ref[...] = y[:, :MIX_WIDTH].astype(BF16)
    v_ref[...] = y[:, MIX_WIDTH:2 * MIX_WIDTH].astype(BF16)

    f = y[:, 2 * MIX_WIDTH:] + bf_ref[...]
    log_f = jnp.minimum(f, 0.0) - jnp.log1p(jnp.exp(-jnp.abs(f)))
    lane = lax.broadcasted_iota(jnp.int32, log_f.shape, 1)
    log_f = jnp.where(lane < N_MIX_HEADS, log_f, 0.0)

    rows = log_f.shape[0]
    r = lax.broadcasted_iota(jnp.int32, (rows, rows), 0)
    c = lax.broadcasted_iota(jnp.int32, (rows, rows), 1)
    tri = (c <= r).astype(BF16)
    hi, mid, lo = _split3(log_f)
    cum = (_dot(tri, hi) + _dot(tri, mid)) + _dot(tri, lo) + carry_ref[...]
    carry_ref[...] = cum[rows - 1:rows, :]

    pieces = jnp.concatenate(_split3(cum), axis=1)
    cq_ref[...] = (_dot(pieces, eq_ref[...]) + oneq_ref[...]).astype(BF16)
    ck_ref[...] = (onek_ref[...] - _dot(pieces, ek_ref[...])).astype(BF16)


def _gate_scatter_constants():
    eq = jnp.zeros((N_PIECES * LANES, LANES), F32)
    ek = jnp.zeros((N_PIECES * LANES, LANES), F32)
    oneq = jnp.zeros((1, LANES), F32)
    onek = jnp.zeros((1, LANES), F32)
    for h in range(N_MIX_HEADS):
        for p in range(N_PIECES):
            eq = eq.at[p * LANES + h, GATE_LANES * h + p].set(1.0)
            ek = ek.at[p * LANES + h, GATE_LANES * h + N_PIECES + p].set(1.0)
            oneq = oneq.at[0, GATE_LANES * h + N_PIECES + p].set(1.0)
            onek = onek.at[0, GATE_LANES * h + p].set(1.0)
    return eq.astype(BF16), ek.astype(BF16), oneq, onek


def _shared_kv(h_rows, g, w_pad, bf_pad, seq_len):
    rows = h_rows.shape[0]
    n = w_pad.shape[1]
    eq, ek, oneq, onek = _gate_scatter_constants()
    kern = functools.partial(_shared_kv_kernel, tiles_per_seq=seq_len // ROW_TILE)
    row_spec = lambda wd: pl.BlockSpec((ROW_TILE, wd), lambda i: (i, 0))
    return pl.pallas_call(
        kern,
        grid=(rows // ROW_TILE,),
        in_specs=[
            row_spec(D_MODEL),
            _const_spec((1, D_MODEL)),
            _const_spec((D_MODEL, n)),
            _const_spec((1, LANES)),
            _const_spec(eq.shape),
            _const_spec(ek.shape),
            _const_spec((1, LANES)),
            _const_spec((1, LANES)),
        ],
        out_specs=[row_spec(MIX_WIDTH), row_spec(MIX_WIDTH), row_spec(LANES), row_spec(LANES)],
        out_shape=[
            jax.ShapeDtypeStruct((rows, MIX_WIDTH), BF16),
            jax.ShapeDtypeStruct((rows, MIX_WIDTH), BF16),
            jax.ShapeDtypeStruct((rows, LANES), BF16),
            jax.ShapeDtypeStruct((rows, LANES), BF16),
        ],
        scratch_shapes=[pltpu.VMEM((1, LANES), F32)],
        compiler_params=pltpu.CompilerParams(
            dimension_semantics=("arbitrary",), vmem_limit_bytes=VMEM_LIMIT),
    )(h_rows, g.reshape(1, D_MODEL), w_pad, bf_pad, eq, ek, oneq, onek)


def _head_lane_masks():
    lane = lax.broadcasted_iota(jnp.int32, (1, LANES), 1)
    return [(lane >= h * HEAD_DIM) & (lane < (h + 1) * HEAD_DIM) for h in range(HEADS_PER_STEP)]


def _stick_kernel(q_ref, k_ref, v_ref, o_ref):
    t0 = pl.program_id(2) * TQ
    masks = _head_lane_masks()
    q2 = q_ref[...]
    zero = jnp.zeros_like(q2)
    q_heads = [jnp.where(m, q2, zero) for m in masks]

    r = lax.broadcasted_iota(jnp.int32, (TK, TK), 0)
    c = lax.broadcasted_iota(jnp.int32, (TK, TK), 1)
    tri = (r >= c).astype(BF16)

    def block(start, state, on_diagonal):
        k_blk = k_ref[pl.ds(start, TK), :]
        v_blk = v_ref[pl.ds(start, TK), :]
        if on_diagonal:
            t_pos = t0 + lax.broadcasted_iota(jnp.int32, (TQ, TK), 0)
            s_pos = start + lax.broadcasted_iota(jnp.int32, (TQ, TK), 1)
            visible = s_pos < t_pos
        new_state = []
        for h in range(HEADS_PER_STEP):
            later, acc = state[h]
            z = _dot_nt(q_heads[h], k_blk)
            sp = jnp.maximum(z, 0.0) + jnp.log1p(jnp.exp(-jnp.abs(z)))
            if on_diagonal:
                sp = jnp.where(visible, sp, 0.0)
            within = _dot(sp.astype(BF16), tri)
            w = jnp.exp(z - within - later)
            if on_diagonal:
                w = jnp.where(visible, w, 0.0)
            acc = acc + _dot(w.astype(BF16), v_blk)
            later = later + within[:, 0:1]
            new_state.append((later, acc))
        return tuple(new_state)

    state = tuple((jnp.zeros((TQ, 1), F32), jnp.zeros((TQ, LANES), F32))
                  for _ in range(HEADS_PER_STEP))
    for d in range(TQ // TK):
        state = block(t0 + TQ - (d + 1) * TK, state, True)

    n_full = t0 // TK

    def body(it, st):
        start = pl.multiple_of((n_full - 1 - it) * TK, TK)
        return block(start, st, False)

    state = lax.fori_loop(0, n_full, body, state)
    out = jnp.where(masks[0], state[0][1], state[1][1])
    o_ref[...] = out.astype(o_ref.dtype)


def _stick_attention(q, k, v, batch, seq_len):
    n_q = seq_len // TQ
    q_spec = pl.BlockSpec((TQ, LANES), lambda b, p, i: (b * n_q + i, p))
    kv_spec = pl.BlockSpec((seq_len, LANES), lambda b, p, i: (b, p))
    return pl.pallas_call(
        _stick_kernel,
        grid=(batch, N_PAIRS, n_q),
        in_specs=[q_spec, kv_spec, kv_spec],
        out_specs=q_spec,
        out_shape=jax.ShapeDtypeStruct(q.shape, BF16),
        compiler_params=pltpu.CompilerParams(
            dimension_semantics=("arbitrary", "arbitrary", "arbitrary"),
            vmem_limit_bytes=VMEM_LIMIT),
    )(q, k, v)


def _fox_kernel(q_ref, cq_ref, k_ref, ck_ref, v_ref, o_ref):
    pair = pl.program_id(1)
    t0 = pl.program_id(2) * TQ
    masks = _head_lane_masks()
    lane = lax.broadcasted_iota(jnp.int32, (1, LANES), 1)
    q2 = q_ref[...]
    cq = cq_ref[...]
    zero = jnp.zeros_like(q2)
    q_heads = []
    for h in range(HEADS_PER_STEP):
        g0 = (pair * HEADS_PER_STEP + h) * GATE_LANES
        gate_mask = (lane >= g0) & (lane < g0 + GATE_LANES)
        q_heads.append(jnp.concatenate(
            [jnp.where(masks[h], q2, zero), jnp.where(gate_mask, cq, zero)], axis=1))
    ones_blk = jnp.ones((TK, LANES), BF16)

    def block(start, state, on_diagonal):
        k_blk = jnp.concatenate([k_ref[pl.ds(start, TK), :], ck_ref[pl.ds(start, TK), :]], axis=1)
        v_blk = jnp.concatenate([v_ref[pl.ds(start, TK), :], ones_blk], axis=1)
        if on_diagonal:
            t_pos = t0 + lax.broadcasted_iota(jnp.int32, (TQ, TK), 0)
            s_pos = start + lax.broadcasted_iota(jnp.int32, (TQ, TK), 1)
            visible = s_pos <= t_pos
        new_state = []
        for h in range(HEADS_PER_STEP):
            m, acc = state[h]
            z = _dot_nt(q_heads[h], k_blk)
            if on_diagonal:
                z = jnp.where(visible, z, NEG_INF)
            m_new = jnp.maximum(m, jnp.max(z, axis=1, keepdims=True))
            p = jnp.exp(z - m_new)
            acc = jnp.exp(m - m_new) * acc + _dot(p.astype(BF16), v_blk)
            new_state.append((m_new, acc))
        return tuple(new_state)

    state = tuple((jnp.full((TQ, 1), NEG_INF, F32), jnp.zeros((TQ, 2 * LANES), F32))
                  for _ in range(HEADS_PER_STEP))
    for d in range(TQ // TK):
        state = block(t0 + TQ - (d + 1) * TK, state, True)

    n_full = t0 // TK

    def body(it, st):
        start = pl.multiple_of((n_full - 1 - it) * TK, TK)
        return block(start, st, False)

    state = lax.fori_loop(0, n_full, body, state)
    outs = [acc[:, :LANES] / acc[:, LANES:] for _, acc in state]
    o_ref[...] = jnp.where(masks[0], outs[0], outs[1]).astype(o_ref.dtype)


def _fox_attention(q, cq, k, ck, v, batch, seq_len):
    n_q = seq_len // TQ
    q_spec = pl.BlockSpec((TQ, LANES), lambda b, p, i: (b * n_q + i, p))
    cq_spec = pl.BlockSpec((TQ, LANES), lambda b, p, i: (b * n_q + i, 0))
    kv_spec = pl.BlockSpec((seq_len, LANES), lambda b, p, i: (b, p))
    ck_spec = pl.BlockSpec((seq_len, LANES), lambda b, p, i: (b, 0))
    return pl.pallas_call(
        _fox_kernel,
        grid=(batch, N_PAIRS, n_q),
        in_specs=[q_spec, cq_spec, kv_spec, ck_spec, kv_spec],
        out_specs=q_spec,
        out_shape=jax.ShapeDtypeStruct(q.shape, BF16),
        compiler_params=pltpu.CompilerParams(
            dimension_semantics=("arbitrary", "arbitrary", "arbitrary"),
            vmem_limit_bytes=VMEM_LIMIT),
    )(q, cq, k, ck, v)


def _post_kernel(h_ref, mix_ref, qm_ref, mkv_ref, wo_ref, g2_ref, w1_ref, w2_ref, gf_ref, o_ref,
                 *, final_norm):
    qm = qm_ref[...]
    mem_k = mkv_ref[0, :, :MEM_WIDTH]
    mem_v = mkv_ref[0, :, MEM_WIDTH:]
    lane = lax.broadcasted_iota(jnp.int32, (1, MEM_WIDTH), 1)
    zero = jnp.zeros_like(qm)
    mem_out = jnp.zeros(qm.shape, F32)
    for hd in range(N_MEM_HEADS):
        head = (lane >= hd * HEAD_DIM) & (lane < (hd + 1) * HEAD_DIM)
        s = _dot_nt(jnp.where(head, qm, zero), mem_k)
        p = jnp.exp(s - jnp.max(s, axis=1, keepdims=True))
        p = p / jnp.sum(p, axis=1, keepdims=True)
        mem_out = jnp.where(head, _dot(p.astype(BF16), mem_v), mem_out)

    merged = jnp.concatenate([mix_ref[...], mem_out.astype(BF16)], axis=1)
    h1 = h_ref[...] + _dot(merged, wo_ref[...])

    xn = _rms_norm_rows(h1, g2_ref[...]).astype(BF16)
    mlp = jnp.zeros(h1.shape, F32)
    for c0 in range(0, D_FF, FF_TILE):
        a = jnp.maximum(_dot(xn, w1_ref[:, c0:c0 + FF_TILE]), 0.0)
        mlp = mlp + _dot((a * a).astype(BF16), w2_ref[c0:c0 + FF_TILE, :])
    h2 = h1 + mlp
    if final_norm:
        h2 = _rms_norm_rows(h2, gf_ref[...])
    o_ref[...] = h2


def _post(h_rows, mix, qm, mkv_l, wo, g2, w1, w2, gf, batch, final_norm):
    rows = h_rows.shape[0]
    tiles_per_seq = rows // batch // ROW_TILE
    n_mem = mkv_l.shape[1]
    row_spec = lambda wd: pl.BlockSpec((ROW_TILE, wd), lambda i: (i, 0))
    single = pl.Buffered(1)
    const = lambda shape: pl.BlockSpec(shape, lambda i: (0,) * len(shape), pipeline_mode=single)
    return pl.pallas_call(
        functools.partial(_post_kernel, final_norm=final_norm),
        grid=(rows // ROW_TILE,),
        in_specs=[
            row_spec(D_MODEL),
            row_spec(MIX_WIDTH),
            row_spec(MEM_WIDTH),
            pl.BlockSpec((1, n_mem, 2 * MEM_WIDTH), lambda i: (i // tiles_per_seq, 0, 0)),
            const(wo.shape),
            const((1, D_MODEL)),
            const(w1.shape),
            const(w2.shape),
            const((1, D_MODEL)),
        ],
        out_specs=row_spec(D_MODEL),
        out_shape=jax.ShapeDtypeStruct(h_rows.shape, F32),
        compiler_params=pltpu.CompilerParams(
            dimension_semantics=("arbitrary",), vmem_limit_bytes=VMEM_LIMIT),
    )(h_rows, mix, qm, mkv_l, wo, g2.reshape(1, D_MODEL), w1, w2, gf.reshape(1, D_MODEL))


def kernel(x, mem, norm1_g, w_in_a, w_in_b, w_mem_kv, mem_norm_g, w_o, norm2_g, w_mlp1, w_mlp2,
           kv_norm_g, w_kv_shared, b_f, final_norm_g):
    batch, seq_len, _ = x.shape
    n_mem = mem.shape[1]
    depth = norm1_g.shape[0]
    n_a = w_in_a.shape[0]
    assert seq_len % ROW_TILE == 0 and seq_len % TQ == 0 and TQ % TK == 0
    scale = HEAD_DIM ** -0.5

    q_scale_a = jnp.concatenate([jnp.full((MIX_WIDTH,), scale, F32), jnp.ones((2 * MIX_WIDTH,), F32),
                                 jnp.full((MEM_WIDTH,), scale, F32)])
    w_a = (w_in_a * q_scale_a).astype(BF16)
    w_b = (w_in_b * scale).astype(BF16)
    w_mkv = w_mem_kv.astype(BF16)
    w_out = w_o.astype(BF16)
    w1 = w_mlp1.astype(BF16)
    w2 = w_mlp2.astype(BF16)
    pad = LANES - N_MIX_HEADS
    w_kv = jnp.pad(w_kv_shared, ((0, 0), (0, pad))).astype(BF16)
    bf_pad = jnp.pad(b_f.astype(F32), (0, pad)).reshape(1, LANES)

    h = x.reshape(batch * seq_len, D_MODEL)
    mkv = _mem_kv(mem.reshape(batch * n_mem, D_MODEL), mem_norm_g, w_mkv)
    mkv = mkv.reshape(depth * batch, n_mem, 2 * MEM_WIDTH)

    k_sh = v_sh = cq = ck = None
    for l in range(depth):
        if l == n_a:
            k_sh, v_sh, cq, ck = _shared_kv(h, kv_norm_g, w_kv, bf_pad, seq_len)
        if l < n_a:
            q, k, v, qm = _norm_proj(h, norm1_g[l], w_a[l],
                                     (MIX_WIDTH, MIX_WIDTH, MIX_WIDTH, MEM_WIDTH))
            mix = _stick_attention(q, k, v, batch, seq_len)
        else:
            q, qm = _norm_proj(h, norm1_g[l], w_b[l - n_a], (MIX_WIDTH, MEM_WIDTH))
            mix = _fox_attention(q, cq, k_sh, ck, v_sh, batch, seq_len)
        h = _post(h, mix, qm, mkv[l * batch:(l + 1) * batch], w_out[l], norm2_g[l], w1[l], w2[l],
                  final_norm_g, batch, final_norm=(l == depth - 1))
    return h.reshape(batch, seq_len, D_MODEL)
```

```python
import functools

import jax
import jax.numpy as jnp
from jax import lax
from jax.experimental import pallas as pl
from jax.experimental.pallas import tpu as pltpu

D_MODEL = 1024
HEAD_DIM = 64
N_MIX_HEADS = 8
N_MEM_HEADS = 4
MIX_WIDTH = N_MIX_HEADS * HEAD_DIM
MEM_WIDTH = N_MEM_HEADS * HEAD_DIM
D_FF = 4 * D_MODEL
EPS = 1e-6
NEG_INF = -1e30

LANES = 128
HEADS_PER_STEP = LANES // HEAD_DIM
N_PAIRS = N_MIX_HEADS // HEADS_PER_STEP
GATE_LANES = 8
N_PIECES = 3

ROW_TILE = 512
TQ = 512
TK = 256
FF_TILE = 1024
VMEM_LIMIT = 56 * 1024 * 1024

BF16 = jnp.bfloat16
F32 = jnp.float32


def _dot(a, b):
    return jnp.dot(a, b, preferred_element_type=F32)


def _dot_nt(a, b):
    return lax.dot_general(a, b, (((1,), (1,)), ((), ())), preferred_element_type=F32)


def _rms_norm_rows(x, g):
    return x * lax.rsqrt(jnp.mean(x * x, axis=-1, keepdims=True) + EPS) * g


def _const_spec(shape):
    return pl.BlockSpec(shape, lambda *_: (0,) * len(shape))


def _norm_proj_kernel(x_ref, g_ref, w_ref, *o_refs):
    xn = _rms_norm_rows(x_ref[...], g_ref[...]).astype(BF16)
    y = _dot(xn, w_ref[...])
    off = 0
    for o_ref in o_refs:
        width = o_ref.shape[-1]
        o_ref[...] = y[:, off:off + width].astype(o_ref.dtype)
        off += width


def _norm_proj(x, g, w, widths):
    rows = x.shape[0]
    n = w.shape[1]
    assert sum(widths) == n and rows % ROW_TILE == 0
    return pl.pallas_call(
        _norm_proj_kernel,
        grid=(rows // ROW_TILE,),
        in_specs=[
            pl.BlockSpec((ROW_TILE, D_MODEL), lambda i: (i, 0)),
            _const_spec((1, D_MODEL)),
            _const_spec((D_MODEL, n)),
        ],
        out_specs=[pl.BlockSpec((ROW_TILE, wd), lambda i: (i, 0)) for wd in widths],
        out_shape=[jax.ShapeDtypeStruct((rows, wd), BF16) for wd in widths],
        compiler_params=pltpu.CompilerParams(
            dimension_semantics=("arbitrary",), vmem_limit_bytes=VMEM_LIMIT),
    )(x, g.reshape(1, D_MODEL), w)


def _mem_kv_kernel(x_ref, g_ref, w_ref, o_ref):
    xn = _rms_norm_rows(x_ref[...], g_ref[0]).astype(BF16)
    o_ref[0] = _dot(xn, w_ref[0]).astype(o_ref.dtype)


def _mem_kv(mem_rows, gains, w):
    depth = w.shape[0]
    rows = mem_rows.shape[0]
    return pl.pallas_call(
        _mem_kv_kernel,
        grid=(depth,),
        in_specs=[
            _const_spec((rows, D_MODEL)),
            pl.BlockSpec((1, 1, D_MODEL), lambda l: (l, 0, 0)),
            pl.BlockSpec((1, D_MODEL, 2 * MEM_WIDTH), lambda l: (l, 0, 0)),
        ],
        out_specs=pl.BlockSpec((1, rows, 2 * MEM_WIDTH), lambda l: (l, 0, 0)),
        out_shape=jax.ShapeDtypeStruct((depth, rows, 2 * MEM_WIDTH), BF16),
        compiler_params=pltpu.CompilerParams(
            dimension_semantics=("arbitrary",), vmem_limit_bytes=VMEM_LIMIT),
    )(mem_rows, gains.reshape(depth, 1, D_MODEL), w)


def _split3(c):
    hi = c.astype(BF16)
    r1 = c - hi.astype(F32)
    mid = r1.astype(BF16)
    lo = (r1 - mid.astype(F32)).astype(BF16)
    return hi, mid, lo


def _shared_kv_kernel(x_ref, g_ref, w_ref, bf_ref, eq_ref, ek_ref, oneq_ref, onek_ref,
                      k_ref, v_ref, cq_ref, ck_ref, carry_ref, *, tiles_per_seq):
    i = pl.program_id(0)

    @pl.when(i % tiles_per_seq == 0)
    def _():
        carry_ref[...] = jnp.zeros_like(carry_ref)

    xn = _rms_norm_rows(x_ref[...], g_ref[...]).astype(BF16)
    y = _dot(xn, w_ref[...])
    k_ref[...] = y[:, :MIX_WIDTH].astype(BF16)
    v_ref[...] = y[:, MIX_WIDTH:2 * MIX_WIDTH].astype(BF16)

    f = y[:, 2 * MIX_WIDTH:] + bf_ref[...]
    log_f = jnp.minimum(f, 0.0) - jnp.log1p(jnp.exp(-jnp.abs(f)))
    lane = lax.broadcasted_iota(jnp.int32, log_f.shape, 1)
    log_f = jnp.where(lane < N_MIX_HEADS, log_f, 0.0)

    rows = log_f.shape[0]
    r = lax.broadcasted_iota(jnp.int32, (rows, rows), 0)
    c = lax.broadcasted_iota(jnp.int32, (rows, rows), 1)
    tri = (c <= r).astype(BF16)
    hi, mid, lo = _split3(log_f)
    cum = (_dot(tri, hi) + _dot(tri, mid)) + _dot(tri, lo) + carry_ref[...]
    carry_ref[...] = cum[rows - 1:rows, :]

    pieces = jnp.concatenate(_split3(cum), axis=1)
    cq_ref[...] = (_dot(pieces, eq_ref[...]) + oneq_ref[...]).astype(BF16)
    ck_ref[...] = (onek_ref[...] - _dot(pieces, ek_ref[...])).astype(BF16)


def _gate_scatter_constants():
    eq = jnp.zeros((N_PIECES * LANES, LANES), F32)
    ek = jnp.zeros((N_PIECES * LANES, LANES), F32)
    oneq = jnp.zeros((1, LANES), F32)
    onek = jnp.zeros((1, LANES), F32)
    for h in range(N_MIX_HEADS):
        for p in range(N_PIECES):
            eq = eq.at[p * LANES + h, GATE_LANES * h + p].set(1.0)
            ek = ek.at[p * LANES + h, GATE_LANES * h + N_PIECES + p].set(1.0)
            oneq = oneq.at[0, GATE_LANES * h + N_PIECES + p].set(1.0)
            onek = onek.at[0, GATE_LANES * h + p].set(1.0)
    return eq.astype(BF16), ek.astype(BF16), oneq, onek


def _shared_kv(h_rows, g, w_pad, bf_pad, seq_len):
    rows = h_rows.shape[0]
    n = w_pad.shape[1]
    eq, ek, oneq, onek = _gate_scatter_constants()
    kern = functools.partial(_shared_kv_kernel, tiles_per_seq=seq_len // ROW_TILE)
    row_spec = lambda wd: pl.BlockSpec((ROW_TILE, wd), lambda i: (i, 0))
    return pl.pallas_call(
        kern,
        grid=(rows // ROW_TILE,),
        in_specs=[
            row_spec(D_MODEL),
            _const_spec((1, D_MODEL)),
            _const_spec((D_MODEL, n)),
            _const_spec((1, LANES)),
            _const_spec(eq.shape),
            _const_spec(ek.shape),
            _const_spec((1, LANES)),
            _const_spec((1, LANES)),
        ],
        out_specs=[row_spec(MIX_WIDTH), row_spec(MIX_WIDTH), row_spec(LANES), row_spec(LANES)],
        out_shape=[
            jax.ShapeDtypeStruct((rows, MIX_WIDTH), BF16),
            jax.ShapeDtypeStruct((rows, MIX_WIDTH), BF16),
            jax.ShapeDtypeStruct((rows, LANES), BF16),
            jax.ShapeDtypeStruct((rows, LANES), BF16),
        ],
        scratch_shapes=[pltpu.VMEM((1, LANES), F32)],
        compiler_params=pltpu.CompilerParams(
            dimension_semantics=("arbitrary",), vmem_limit_bytes=VMEM_LIMIT),
    )(h_rows, g.reshape(1, D_MODEL), w_pad, bf_pad, eq, ek, oneq, onek)


def _head_lane_masks():
    lane = lax.broadcasted_iota(jnp.int32, (1, LANES), 1)
    return [(lane >= h * HEAD_DIM) & (lane < (h + 1) * HEAD_DIM) for h in range(HEADS_PER_STEP)]


def _stack_heads(x2, masks):
    zero = jnp.zeros_like(x2)
    return jnp.concatenate([jnp.where(m, x2, zero) for m in masks], axis=0)


def _heads_to_lanes(x):
    n = x.shape[0] // HEADS_PER_STEP
    return jnp.concatenate([x[h * n:(h + 1) * n] for h in range(HEADS_PER_STEP)], axis=1)


def _visible(t0, start, strict):
    t_pos = t0 + (lax.broadcasted_iota(jnp.int32, (HEADS_PER_STEP * TQ, TK), 0) & (TQ - 1))
    s_pos = start + lax.broadcasted_iota(jnp.int32, (HEADS_PER_STEP * TQ, TK), 1)
    return s_pos < t_pos if strict else s_pos <= t_pos


def _stick_kernel(q_ref, k_ref, v_ref, o_ref, q_scr, tri_scr, z_scr, zl_scr, w_scr, later_scr, acc_scr):
    t0 = pl.program_id(2) * TQ
    masks = _head_lane_masks()
    q_scr[...] = _stack_heads(q_ref[...], masks)
    r = lax.broadcasted_iota(jnp.int32, (TK, TK), 0)
    c = lax.broadcasted_iota(jnp.int32, (TK, TK), 1)
    tri_scr[...] = (r >= c).astype(BF16)
    later_scr[...] = jnp.zeros_like(later_scr)
    acc_scr[...] = jnp.zeros_like(acc_scr)

    def scores(start):
        z_scr[...] = _dot_nt(q_scr[...], k_ref[pl.ds(start, TK), :])

    def weights(visible=None):
        z = z_scr[...]
        neg_abs = lax.bitcast_convert_type(
            lax.bitcast_convert_type(z, jnp.uint32) | jnp.uint32(0x80000000), F32)
        sp = jnp.maximum(z, 0.0) + jnp.log(1.0 + jnp.exp(neg_abs))
        if visible is not None:
            sp = jnp.where(visible, sp, 0.0)
        zl_scr[...] = z - later_scr[...]
        within = _dot(sp.astype(BF16), tri_scr[...])
        w = jnp.exp(zl_scr[...] - within)
        if visible is not None:
            w = jnp.where(visible, w, 0.0)
        w_scr[...] = _heads_to_lanes(w.astype(BF16))
        later_scr[...] += within[:, 0:1]

    def values(start):
        acc_scr[...] += _dot(w_scr[...], _stack_heads(v_ref[pl.ds(start, TK), :], masks))

    for d in range(TQ // TK):
        start = t0 + TQ - (d + 1) * TK
        scores(start)
        weights(_visible(t0, start, strict=True))
        if d + 1 < TQ // TK:
            values(start)

    n_full = t0 // TK
    scores(pl.multiple_of(jnp.maximum(t0 - TK, 0), TK))

    @pl.loop(0, n_full)
    def _(it):
        start = pl.multiple_of((n_full - 1 - it) * TK, TK)
        values(start + TK)
        weights()
        scores(pl.multiple_of(jnp.maximum(start - TK, 0), TK))

    values(0)
    o_ref[...] = acc_scr[...].astype(o_ref.dtype)


def _stick_attention(q, k, v, batch, seq_len):
    n_q = seq_len // TQ
    q_spec = pl.BlockSpec((TQ, LANES), lambda b, p, i: (b * n_q + i, p))
    kv_spec = pl.BlockSpec((seq_len, LANES), lambda b, p, i: (b, p))
    return pl.pallas_call(
        _stick_kernel,
        grid=(batch, N_PAIRS, n_q),
        in_specs=[q_spec, kv_spec, kv_spec],
        out_specs=q_spec,
        out_shape=jax.ShapeDtypeStruct(q.shape, BF16),
        scratch_shapes=[
            pltpu.VMEM((HEADS_PER_STEP * TQ, LANES), BF16),
            pltpu.VMEM((TK, TK), BF16),
            pltpu.VMEM((HEADS_PER_STEP * TQ, TK), F32),
            pltpu.VMEM((HEADS_PER_STEP * TQ, TK), F32),
            pltpu.VMEM((TQ, HEADS_PER_STEP * TK), BF16),
            pltpu.VMEM((HEADS_PER_STEP * TQ, 1), F32),
            pltpu.VMEM((TQ, LANES), F32),
        ],
        compiler_params=pltpu.CompilerParams(
            dimension_semantics=("arbitrary", "arbitrary", "arbitrary"),
            vmem_limit_bytes=VMEM_LIMIT),
    )(q, k, v)


def _fox_kernel(q_ref, cq_ref, k_ref, ck_ref, v_ref, o_ref,
                q_scr, ones_scr, z_scr, p_scr, alpha_scr, m_scr, acc_scr):
    pair = pl.program_id(1)
    t0 = pl.program_id(2) * TQ
    masks = _head_lane_masks()
    lane = lax.broadcasted_iota(jnp.int32, (1, LANES), 1)
    cq = cq_ref[...]
    gate_rows = []
    for h in range(HEADS_PER_STEP):
        g0 = (pair * HEADS_PER_STEP + h) * GATE_LANES
        gate_rows.append(jnp.where((lane >= g0) & (lane < g0 + GATE_LANES), cq, jnp.zeros_like(cq)))
    q_scr[...] = jnp.concatenate([_stack_heads(q_ref[...], masks), jnp.concatenate(gate_rows, axis=0)],
                                 axis=1)
    lane_rows = lax.broadcasted_iota(jnp.int32, (HEADS_PER_STEP * TK, LANES), 1) // HEAD_DIM
    head_rows = lax.broadcasted_iota(jnp.int32, (HEADS_PER_STEP * TK, LANES), 0) // TK
    ones_scr[...] = (lane_rows == head_rows).astype(F32).astype(BF16)

    def scores(start):
        k_blk = jnp.concatenate([k_ref[pl.ds(start, TK), :], ck_ref[pl.ds(start, TK), :]], axis=1)
        z_scr[...] = _dot_nt(q_scr[...], k_blk)

    def probs(visible=None):
        z = z_scr[...]
        if visible is not None:
            z = jnp.where(visible, z, NEG_INF)
        m = m_scr[...]
        m_new = jnp.maximum(m, jnp.max(z, axis=1, keepdims=True))
        m_scr[...] = m_new
        shift = jnp.concatenate([m_new] * (TK // LANES), axis=1)
        p_scr[...] = _heads_to_lanes(jnp.exp(z - shift).astype(BF16))
        alpha = jnp.exp(m - m_new)
        alpha_scr[...] = jnp.where(masks[0], alpha[:TQ], alpha[TQ:])

    def accumulate(start):
        v_rows = jnp.concatenate([_stack_heads(v_ref[pl.ds(start, TK), :], masks), ones_scr[...]],
                                 axis=1)
        alpha = alpha_scr[...]
        acc_scr[...] = (jnp.concatenate([alpha, alpha], axis=1) * acc_scr[...]
                        + _dot(p_scr[...], v_rows))

    n_full = t0 // TK
    scores(0)
    p_scr[...] = jnp.zeros_like(p_scr)
    alpha_scr[...] = jnp.ones_like(alpha_scr)
    m_scr[...] = jnp.full_like(m_scr, NEG_INF)
    acc_scr[...] = jnp.zeros_like(acc_scr)

    @pl.loop(0, n_full)
    def _(it):
        start = pl.multiple_of(it * TK, TK)
        accumulate(pl.multiple_of(jnp.maximum(start - TK, 0), TK))
        probs()
        scores(start + TK)

    prev_start = pl.multiple_of(jnp.maximum(t0 - TK, 0), TK)
    for d in range(TQ // TK):
        start = t0 + d * TK
        accumulate(prev_start)
        probs(_visible(t0, start, strict=False))
        prev_start = start
        if d + 1 < TQ // TK:
            scores(start + TK)
    accumulate(prev_start)
    acc = acc_scr[...]
    o_ref[...] = (acc[:, :LANES] / acc[:, LANES:]).astype(o_ref.dtype)


def _fox_attention(q, cq, k, ck, v, batch, seq_len):
    n_q = seq_len // TQ
    q_spec = pl.BlockSpec((TQ, LANES), lambda b, p, i: (b * n_q + i, p))
    cq_spec = pl.BlockSpec((TQ, LANES), lambda b, p, i: (b * n_q + i, 0))
    kv_spec = pl.BlockSpec((seq_len, LANES), lambda b, p, i: (b, p))
    ck_spec = pl.BlockSpec((seq_len, LANES), lambda b, p, i: (b, 0))
    return pl.pallas_call(
        _fox_kernel,
        grid=(batch, N_PAIRS, n_q),
        in_specs=[q_spec, cq_spec, kv_spec, ck_spec, kv_spec],
        out_specs=q_spec,
        out_shape=jax.ShapeDtypeStruct(q.shape, BF16),
        scratch_shapes=[
            pltpu.VMEM((HEADS_PER_STEP * TQ, 2 * LANES), BF16),
            pltpu.VMEM((HEADS_PER_STEP * TK, LANES), BF16),
            pltpu.VMEM((HEADS_PER_STEP * TQ, TK), F32),
            pltpu.VMEM((TQ, HEADS_PER_STEP * TK), BF16),
            pltpu.VMEM((TQ, LANES), F32),
            pltpu.VMEM((HEADS_PER_STEP * TQ, LANES), F32),
            pltpu.VMEM((TQ, 2 * LANES), F32),
        ],
        compiler_params=pltpu.CompilerParams(
            dimension_semantics=("arbitrary", "arbitrary", "arbitrary"),
            vmem_limit_bytes=VMEM_LIMIT),
    )(q, cq, k, ck, v)


def _post_kernel(h_ref, mix_ref, qm_ref, mkv_ref, wo_ref, g2_ref, w1_ref, w2_ref, gf_ref, o_ref,
                 *, final_norm):
    qm = qm_ref[...]
    mem_k = mkv_ref[0, :, :MEM_WIDTH]
    mem_v = mkv_ref[0, :, MEM_WIDTH:]
    lane = lax.broadcasted_iota(jnp.int32, (1, MEM_WIDTH), 1)
    zero = jnp.zeros_like(qm)
    mem_out = jnp.zeros(qm.shape, F32)
    for hd in range(N_MEM_HEADS):
        head = (lane >= hd * HEAD_DIM) & (lane < (hd + 1) * HEAD_DIM)
        s = _dot_nt(jnp.where(head, qm, zero), mem_k)
        p = jnp.exp(s - jnp.max(s, axis=1, keepdims=True))
        p = p / jnp.sum(p, axis=1, keepdims=True)
        mem_out = jnp.where(head, _dot(p.astype(BF16), mem_v), mem_out)

    merged = jnp.concatenate([mix_ref[...], mem_out.astype(BF16)], axis=1)
    h1 = h_ref[...] + _dot(merged, wo_ref[...])

    xn = _rms_norm_rows(h1, g2_ref[...]).astype(BF16)
    mlp = jnp.zeros(h1.shape, F32)
    for c0 in range(0, D_FF, FF_TILE):
        a = jnp.maximum(_dot(xn, w1_ref[:, c0:c0 + FF_TILE]), 0.0)
        mlp = mlp + _dot((a * a).astype(BF16), w2_ref[c0:c0 + FF_TILE, :])
    h2 = h1 + mlp
    if final_norm:
        h2 = _rms_norm_rows(h2, gf_ref[...])
    o_ref[...] = h2


def _post(h_rows, mix, qm, mkv_l, wo, g2, w1, w2, gf, batch, final_norm):
    rows = h_rows.shape[0]
    tiles_per_seq = rows // batch // ROW_TILE
    n_mem = mkv_l.shape[1]
    row_spec = lambda wd: pl.BlockSpec((ROW_TILE, wd), lambda i: (i, 0))
    single = pl.Buffered(1)
    const = lambda shape: pl.BlockSpec(shape, lambda i: (0,) * len(shape), pipeline_mode=single)
    return pl.pallas_call(
        functools.partial(_post_kernel, final_norm=final_norm),
        grid=(rows // ROW_TILE,),
        in_specs=[
            row_spec(D_MODEL),
            row_spec(MIX_WIDTH),
            row_spec(MEM_WIDTH),
            pl.BlockSpec((1, n_mem, 2 * MEM_WIDTH), lambda i: (i // tiles_per_seq, 0, 0)),
            const(wo.shape),
            const((1, D_MODEL)),
            const(w1.shape),
            const(w2.shape),
            const((1, D_MODEL)),
        ],
        out_specs=row_spec(D_MODEL),
        out_shape=jax.ShapeDtypeStruct(h_rows.shape, F32),
        compiler_params=pltpu.CompilerParams(
            dimension_semantics=("arbitrary",), vmem_limit_bytes=VMEM_LIMIT),
    )(h_rows, mix, qm, mkv_l, wo, g2.reshape(1, D_MODEL), w1, w2, gf.reshape(1, D_MODEL))


def kernel(x, mem, norm1_g, w_in_a, w_in_b, w_mem_kv, mem_norm_g, w_o, norm2_g, w_mlp1, w_mlp2,
           kv_norm_g, w_kv_shared, b_f, final_norm_g):
    batch, seq_len, _ = x.shape
    n_mem = mem.shape[1]
    depth = norm1_g.shape[0]
    n_a = w_in_a.shape[0]
    assert seq_len % ROW_TILE == 0 and seq_len % TQ == 0 and TQ % TK == 0
    scale = HEAD_DIM ** -0.5

    q_scale_a = jnp.concatenate([jnp.full((MIX_WIDTH,), scale, F32), jnp.ones((2 * MIX_WIDTH,), F32),
                                 jnp.full((MEM_WIDTH,), scale, F32)])
    w_a = (w_in_a * q_scale_a).astype(BF16)
    w_b = (w_in_b * scale).astype(BF16)
    w_mkv = w_mem_kv.astype(BF16)
    w_out = w_o.astype(BF16)
    w1 = w_mlp1.astype(BF16)
    w2 = w_mlp2.astype(BF16)
    pad = LANES - N_MIX_HEADS
    w_kv = jnp.pad(w_kv_shared, ((0, 0), (0, pad))).astype(BF16)
    bf_pad = jnp.pad(b_f.astype(F32), (0, pad)).reshape(1, LANES)

    h = x.reshape(batch * seq_len, D_MODEL)
    mkv = _mem_kv(mem.reshape(batch * n_mem, D_MODEL), mem_norm_g, w_mkv)
    mkv = mkv.reshape(depth * batch, n_mem, 2 * MEM_WIDTH)

    k_sh = v_sh = cq = ck = None
    for l in range(depth):
        if l == n_a:
            k_sh, v_sh, cq, ck = _shared_kv(h, kv_norm_g, w_kv, bf_pad, seq_len)
        if l < n_a:
            q, k, v, qm = _norm_proj(h, norm1_g[l], w_a[l],
                                     (MIX_WIDTH, MIX_WIDTH, MIX_WIDTH, MEM_WIDTH))
            mix = _stick_attention(q, k, v, batch, seq_len)
        else:
            q, qm = _norm_proj(h, norm1_g[l], w_b[l - n_a], (MIX_WIDTH, MEM_WIDTH))
            mix = _fox_attention(q, cq, k_sh, ck, v_sh, batch, seq_len)
        h = _post(h, mix, qm, mkv[l * batch:(l + 1) * batch], w_out[l], norm2_g[l], w1[l], w2[l],
                  final_norm_g, batch, final_norm=(l == depth - 1))
    return h.reshape(batch, seq_len, D_MODEL)
```

```python
import functools

import jax
import jax.numpy as jnp
from jax import lax
from jax.experimental import pallas as pl
from jax.experimental.pallas import tpu as pltpu

D_MODEL = 1024
HEAD_DIM = 64
N_MIX_HEADS = 8
N_MEM_HEADS = 4
MIX_WIDTH = N_MIX_HEADS * HEAD_DIM
MEM_WIDTH = N_MEM_HEADS * HEAD_DIM
D_FF = 4 * D_MODEL
EPS = 1e-6
NEG_INF = -1e30
LOG2E = 1.4426950408889634

LANES = 128
HEADS_PER_STEP = LANES // HEAD_DIM
N_PAIRS = N_MIX_HEADS // HEADS_PER_STEP
GATE_LANES = 8
N_PIECES = 3

ROW_TILE = 512
TQ = 512
TK = 256
FF_TILE = 1024
VMEM_LIMIT = 56 * 1024 * 1024

BF16 = jnp.bfloat16
F32 = jnp.float32


def _dot(a, b):
    return jnp.dot(a, b, preferred_element_type=F32)


def _dot_nt(a, b):
    return lax.dot_general(a, b, (((1,), (1,)), ((), ())), preferred_element_type=F32)


def _rms_norm_rows(x, g):
    return x * lax.rsqrt(jnp.mean(x * x, axis=-1, keepdims=True) + EPS) * g


def _const_spec(shape):
    return pl.BlockSpec(shape, lambda *_: (0,) * len(shape))


def _slot_pair(shape, dtype):
    return (pltpu.VMEM(shape, dtype), pltpu.VMEM(shape, dtype))


def _norm_proj_kernel(x_ref, g_ref, w_ref, *o_refs, scales):
    xn = _rms_norm_rows(x_ref[...], g_ref[...]).astype(BF16)
    y = _dot(xn, w_ref[...])
    off = 0
    for o_ref, scale in zip(o_refs, scales):
        width = o_ref.shape[-1]
        part = y[:, off:off + width]
        o_ref[...] = (part if scale is None else part * scale).astype(o_ref.dtype)
        off += width


def _norm_proj(x, g, w, widths, scales=None):
    rows = x.shape[0]
    n = w.shape[1]
    assert sum(widths) == n and rows % ROW_TILE == 0
    scales = (None,) * len(widths) if scales is None else scales
    return pl.pallas_call(
        functools.partial(_norm_proj_kernel, scales=scales),
        grid=(rows // ROW_TILE,),
        in_specs=[
            pl.BlockSpec((ROW_TILE, D_MODEL), lambda i: (i, 0)),
            _const_spec((1, D_MODEL)),
            _const_spec((D_MODEL, n)),
        ],
        out_specs=[pl.BlockSpec((ROW_TILE, wd), lambda i: (i, 0)) for wd in widths],
        out_shape=[jax.ShapeDtypeStruct((rows, wd), BF16) for wd in widths],
        compiler_params=pltpu.CompilerParams(
            dimension_semantics=("arbitrary",), vmem_limit_bytes=VMEM_LIMIT),
    )(x, g.reshape(1, D_MODEL), w)


def _mem_kv_kernel(x_ref, g_ref, w_ref, o_ref):
    xn = _rms_norm_rows(x_ref[...], g_ref[0]).astype(BF16)
    o_ref[0] = _dot(xn, w_ref[0]).astype(o_ref.dtype)


def _mem_kv(mem_rows, gains, w):
    depth = w.shape[0]
    rows = mem_rows.shape[0]
    return pl.pallas_call(
        _mem_kv_kernel,
        grid=(depth,),
        in_specs=[
            _const_spec((rows, D_MODEL)),
            pl.BlockSpec((1, 1, D_MODEL), lambda l: (l, 0, 0)),
            pl.BlockSpec((1, D_MODEL, 2 * MEM_WIDTH), lambda l: (l, 0, 0)),
        ],
        out_specs=pl.BlockSpec((1, rows, 2 * MEM_WIDTH), lambda l: (l, 0, 0)),
        out_shape=jax.ShapeDtypeStruct((depth, rows, 2 * MEM_WIDTH), BF16),
        compiler_params=pltpu.CompilerParams(
            dimension_semantics=("arbitrary",), vmem_limit_bytes=VMEM_LIMIT),
    )(mem_rows, gains.reshape(depth, 1, D_MODEL), w)


def _split3(c):
    hi = c.astype(BF16)
    r1 = c - hi.astype(F32)
    mid = r1.astype(BF16)
    lo = (r1 - mid.astype(F32)).astype(BF16)
    return hi, mid, lo


def _shared_kv_kernel(x_ref, g_ref, w_ref, bf_ref, eq_ref, ek_ref, oneq_ref, onek_ref,
                      k_ref, v_ref, cq_ref, ck_ref, carry_ref, *, tiles_per_seq):
    i = pl.program_id(0)

    @pl.when(i % tiles_per_seq == 0)
    def _():
        carry_ref[...] = jnp.zeros_like(carry_ref)

    xn = _rms_norm_rows(x_ref[...], g_ref[...]).astype(BF16)
    y = _dot(xn, w_ref[...])
    k_ref[...] = y[:, :MIX_WIDTH].astype(BF16)
    v_ref[...] = y[:, MIX_WIDTH:2 * MIX_WIDTH].astype(BF16)

    f = y[:, 2 * MIX_WIDTH:] + bf_ref[...]
    log_f = jnp.minimum(f, 0.0) - jnp.log1p(jnp.exp(-jnp.abs(f)))
    lane = lax.broadcasted_iota(jnp.int32, log_f.shape, 1)
    log_f = jnp.where(lane < N_MIX_HEADS, log_f, 0.0)

    rows = log_f.shape[0]
    r = lax.broadcasted_iota(jnp.int32, (rows, rows), 0)
    c = lax.broadcasted_iota(jnp.int32, (rows, rows), 1)
    tri = (c <= r).astype(BF16)
    hi, mid, lo = _split3(log_f)
    cum = (_dot(tri, hi) + _dot(tri, mid)) + _dot(tri, lo) + carry_ref[...]
    carry_ref[...] = cum[rows - 1:rows, :]

    pieces = jnp.concatenate(_split3(cum), axis=1)
    cq_ref[...] = (_dot(pieces, eq_ref[...]) + oneq_ref[...]).astype(BF16)
    ck_ref[...] = (onek_ref[...] - _dot(pieces, ek_ref[...])).astype(BF16)


def _gate_scatter_constants():
    eq = jnp.zeros((N_PIECES * LANES, LANES), F32)
    ek = jnp.zeros((N_PIECES * LANES, LANES), F32)
    oneq = jnp.zeros((1, LANES), F32)
    onek = jnp.zeros((1, LANES), F32)
    for h in range(N_MIX_HEADS):
        for p in range(N_PIECES):
            eq = eq.at[p * LANES + h, GATE_LANES * h + p].set(1.0)
            ek = ek.at[p * LANES + h, GATE_LANES * h + N_PIECES + p].set(1.0)
            oneq = oneq.at[0, GATE_LANES * h + N_PIECES + p].set(1.0)
            onek = onek.at[0, GATE_LANES * h + p].set(1.0)
    return eq.astype(BF16), ek.astype(BF16), oneq, onek


def _shared_kv(h_rows, g, w_pad, bf_pad, seq_len):
    rows = h_rows.shape[0]
    n = w_pad.shape[1]
    eq, ek, oneq, onek = _gate_scatter_constants()
    kern = functools.partial(_shared_kv_kernel, tiles_per_seq=seq_len // ROW_TILE)
    row_spec = lambda wd: pl.BlockSpec((ROW_TILE, wd), lambda i: (i, 0))
    return pl.pallas_call(
        kern,
        grid=(rows // ROW_TILE,),
        in_specs=[
            row_spec(D_MODEL),
            _const_spec((1, D_MODEL)),
            _const_spec((D_MODEL, n)),
            _const_spec((1, LANES)),
            _const_spec(eq.shape),
            _const_spec(ek.shape),
            _const_spec((1, LANES)),
            _const_spec((1, LANES)),
        ],
        out_specs=[row_spec(MIX_WIDTH), row_spec(MIX_WIDTH), row_spec(LANES), row_spec(LANES)],
        out_shape=[
            jax.ShapeDtypeStruct((rows, MIX_WIDTH), BF16),
            jax.ShapeDtypeStruct((rows, MIX_WIDTH), BF16),
            jax.ShapeDtypeStruct((rows, LANES), BF16),
            jax.ShapeDtypeStruct((rows, LANES), BF16),
        ],
        scratch_shapes=[pltpu.VMEM((1, LANES), F32)],
        compiler_params=pltpu.CompilerParams(
            dimension_semantics=("arbitrary",), vmem_limit_bytes=VMEM_LIMIT),
    )(h_rows, g.reshape(1, D_MODEL), w_pad, bf_pad, eq, ek, oneq, onek)


def _head_lane_masks():
    lane = lax.broadcasted_iota(jnp.int32, (1, LANES), 1)
    return [(lane >= h * HEAD_DIM) & (lane < (h + 1) * HEAD_DIM) for h in range(HEADS_PER_STEP)]


def _stack_heads(x2, masks):
    zero = jnp.zeros_like(x2)
    return jnp.concatenate([jnp.where(m, x2, zero) for m in masks], axis=0)


def _heads_to_lanes(x):
    n = x.shape[0] // HEADS_PER_STEP
    return jnp.concatenate([x[h * n:(h + 1) * n] for h in range(HEADS_PER_STEP)], axis=1)


def _visible(t0, start, width, strict):
    t_pos = t0 + (lax.broadcasted_iota(jnp.int32, (HEADS_PER_STEP * TQ, width), 0) & (TQ - 1))
    s_pos = start + lax.broadcasted_iota(jnp.int32, (HEADS_PER_STEP * TQ, width), 1)
    return s_pos < t_pos if strict else s_pos <= t_pos


def _stick_kernel(q_ref, k_ref, v_ref, o_ref, q_scr, tri_scr, z_scr, zl_scr, w_scr, later_scr, acc_scr):
    t0 = pl.program_id(2) * TQ
    masks = _head_lane_masks()
    q_scr[...] = _stack_heads(q_ref[...], masks)
    r = lax.broadcasted_iota(jnp.int32, (TK, TK), 0)
    c = lax.broadcasted_iota(jnp.int32, (TK, TK), 1)
    tri_scr[...] = (r >= c).astype(BF16)
    later_scr[...] = jnp.zeros_like(later_scr)
    acc_scr[...] = jnp.zeros_like(acc_scr)

    def key_start(n):
        return pl.multiple_of(jnp.maximum(t0 + (TQ // TK - 1 - n) * TK, 0), TK)

    def visible(n):
        return _visible(t0, t0 + (TQ // TK - 1 - n) * TK, TK, strict=True)

    def scores(n, slot):
        z_scr[slot][...] = _dot_nt(q_scr[...], k_ref[pl.ds(key_start(n), TK), :])

    def weights(slot, mask=None):
        z = z_scr[slot][...]
        neg_abs = lax.bitcast_convert_type(
            lax.bitcast_convert_type(z, jnp.uint32) | jnp.uint32(0x80000000), F32)
        sp = jnp.maximum(z, 0.0) + jnp.log(1.0 + jnp.exp2(neg_abs)) * LOG2E
        if mask is not None:
            sp = jnp.where(mask, sp, 0.0)
        zl_scr[...] = z - later_scr[...]
        within = _dot(sp.astype(BF16), tri_scr[...])
        w = jnp.exp2(zl_scr[...] - within)
        if mask is not None:
            w = jnp.where(mask, w, 0.0)
        w_scr[slot][...] = _heads_to_lanes(w.astype(BF16))
        later_scr[...] += within[:, 0:1]

    def values(n, slot):
        acc_scr[...] += _dot(w_scr[slot][...], _stack_heads(v_ref[pl.ds(key_start(n), TK), :], masks))

    assert TQ == 2 * TK
    n_blocks = TQ // TK + t0 // TK
    scores(0, 0)
    weights(0, visible(0)); scores(1, 1)
    values(0, 0); weights(1, visible(1)); scores(2, 0)

    @pl.loop(1, n_blocks - 2, step=2)
    def _(n):
        scores(n + 2, 1); values(n, 1); weights(0)
        scores(n + 3, 0); values(n + 1, 0); weights(1)

    values(n_blocks - 1, 1)
    o_ref[...] = acc_scr[...].astype(o_ref.dtype)


def _stick_attention(q, k, v, batch, seq_len):
    n_q = seq_len // TQ
    q_spec = pl.BlockSpec((TQ, LANES), lambda b, p, i: (b * n_q + i, p))
    kv_spec = pl.BlockSpec((seq_len, LANES), lambda b, p, i: (b, p))
    return pl.pallas_call(
        _stick_kernel,
        grid=(batch, N_PAIRS, n_q),
        in_specs=[q_spec, kv_spec, kv_spec],
        out_specs=q_spec,
        out_shape=jax.ShapeDtypeStruct(q.shape, BF16),
        scratch_shapes=[
            pltpu.VMEM((HEADS_PER_STEP * TQ, LANES), BF16),
            pltpu.VMEM((TK, TK), BF16),
            _slot_pair((HEADS_PER_STEP * TQ, TK), F32),
            pltpu.VMEM((HEADS_PER_STEP * TQ, TK), F32),
            _slot_pair((TQ, HEADS_PER_STEP * TK), BF16),
            pltpu.VMEM((HEADS_PER_STEP * TQ, 1), F32),
            pltpu.VMEM((TQ, LANES), F32),
        ],
        compiler_params=pltpu.CompilerParams(
            dimension_semantics=("arbitrary", "arbitrary", "arbitrary"),
            vmem_limit_bytes=VMEM_LIMIT),
    )(q, k, v)


def _fox_kernel(q_ref, cq_ref, k_ref, ck_ref, v_ref, o_ref,
                q_scr, ones_scr, z_scr, p_scr, alpha_scr, m_scr, acc_scr):
    pair = pl.program_id(1)
    t0 = pl.program_id(2) * TQ
    masks = _head_lane_masks()
    lane = lax.broadcasted_iota(jnp.int32, (1, LANES), 1)
    cq = cq_ref[...]
    gate_rows = []
    for h in range(HEADS_PER_STEP):
        g0 = (pair * HEADS_PER_STEP + h) * GATE_LANES
        gate_rows.append(jnp.where((lane >= g0) & (lane < g0 + GATE_LANES), cq, jnp.zeros_like(cq)))
    q_scr[...] = jnp.concatenate([_stack_heads(q_ref[...], masks), jnp.concatenate(gate_rows, axis=0)],
                                 axis=1)
    lane_rows = lax.broadcasted_iota(jnp.int32, (HEADS_PER_STEP * TK, LANES), 1) // HEAD_DIM
    head_rows = lax.broadcasted_iota(jnp.int32, (HEADS_PER_STEP * TK, LANES), 0) // TK
    ones_scr[...] = (lane_rows == head_rows).astype(F32).astype(BF16)
    m_scr[...] = jnp.full_like(m_scr, NEG_INF)
    acc_scr[...] = jnp.zeros_like(acc_scr)
    last_start = k_ref.shape[0] - TK

    def key_start(n):
        return pl.multiple_of(jnp.minimum(n * TK, last_start), TK)

    def visible(n):
        return _visible(t0, n * TK, TK, strict=False)

    def scores(n, slot):
        start = key_start(n)
        k_blk = jnp.concatenate([k_ref[pl.ds(start, TK), :], ck_ref[pl.ds(start, TK), :]], axis=1)
        z_scr[slot][...] = _dot_nt(q_scr[...], k_blk)

    def probs(slot, mask=None):
        z = z_scr[slot][...]
        if mask is not None:
            z = jnp.where(mask, z, NEG_INF)
        m = m_scr[...]
        m_new = jnp.maximum(m, jnp.max(z, axis=1, keepdims=True))
        m_scr[...] = m_new
        shift = jnp.concatenate([m_new] * (TK // LANES), axis=1)
        p_scr[slot][...] = _heads_to_lanes(jnp.exp(z - shift).astype(BF16))
        alpha = jnp.exp(m - m_new)
        alpha_scr[slot][...] = jnp.where(masks[0], alpha[:TQ], alpha[TQ:])

    def accumulate(n, slot):
        v_rows = jnp.concatenate(
            [_stack_heads(v_ref[pl.ds(key_start(n), TK), :], masks), ones_scr[...]], axis=1)
        alpha = alpha_scr[slot][...]
        acc_scr[...] = (jnp.concatenate([alpha, alpha], axis=1) * acc_scr[...]
                        + _dot(p_scr[slot][...], v_rows))

    n_blocks = TQ // TK + t0 // TK
    scores(0, 0)
    probs(0, visible(0)); scores(1, 1)
    accumulate(0, 0); probs(1, visible(1)); scores(2, 0)

    @pl.loop(1, n_blocks - 3, step=2)
    def _(n):
        scores(n + 2, 1); accumulate(n, 1); probs(0)
        scores(n + 3, 0); accumulate(n + 1, 0); probs(1)

    @pl.when(n_blocks >= 4)
    def _():
        accumulate(n_blocks - 3, 1); probs(0, visible(n_blocks - 2)); scores(n_blocks - 1, 1)
        accumulate(n_blocks - 2, 0); probs(1, visible(n_blocks - 1))

    accumulate(n_blocks - 1, 1)
    acc = acc_scr[...]
    o_ref[...] = (acc[:, :LANES] / acc[:, LANES:]).astype(o_ref.dtype)


def _fox_attention(q, cq, k, ck, v, batch, seq_len):
    n_q = seq_len // TQ
    q_spec = pl.BlockSpec((TQ, LANES), lambda b, p, i: (b * n_q + i, p))
    cq_spec = pl.BlockSpec((TQ, LANES), lambda b, p, i: (b * n_q + i, 0))
    kv_spec = pl.BlockSpec((seq_len, LANES), lambda b, p, i: (b, p))
    ck_spec = pl.BlockSpec((seq_len, LANES), lambda b, p, i: (b, 0))
    return pl.pallas_call(
        _fox_kernel,
        grid=(batch, N_PAIRS, n_q),
        in_specs=[q_spec, cq_spec, kv_spec, ck_spec, kv_spec],
        out_specs=q_spec,
        out_shape=jax.ShapeDtypeStruct(q.shape, BF16),
        scratch_shapes=[
            pltpu.VMEM((HEADS_PER_STEP * TQ, 2 * LANES), BF16),
            pltpu.VMEM((HEADS_PER_STEP * TK, LANES), BF16),
            _slot_pair((HEADS_PER_STEP * TQ, TK), F32),
            _slot_pair((TQ, HEADS_PER_STEP * TK), BF16),
            _slot_pair((TQ, LANES), F32),
            pltpu.VMEM((HEADS_PER_STEP * TQ, LANES), F32),
            pltpu.VMEM((TQ, 2 * LANES), F32),
        ],
        compiler_params=pltpu.CompilerParams(
            dimension_semantics=("arbitrary", "arbitrary", "arbitrary"),
            vmem_limit_bytes=VMEM_LIMIT),
    )(q, cq, k, ck, v)


def _post_kernel(h_ref, mix_ref, qm_ref, mkv_ref, wo_ref, g2_ref, w1_ref, w2_ref, gf_ref, o_ref,
                 *, final_norm):
    qm = qm_ref[...]
    mem_k = mkv_ref[0, :, :MEM_WIDTH]
    mem_v = mkv_ref[0, :, MEM_WIDTH:]
    lane = lax.broadcasted_iota(jnp.int32, (1, MEM_WIDTH), 1)
    zero = jnp.zeros_like(qm)
    mem_out = jnp.zeros(qm.shape, F32)
    for hd in range(N_MEM_HEADS):
        head = (lane >= hd * HEAD_DIM) & (lane < (hd + 1) * HEAD_DIM)
        s = _dot_nt(jnp.where(head, qm, zero), mem_k)
        p = jnp.exp(s - jnp.max(s, axis=1, keepdims=True))
        p = p / jnp.sum(p, axis=1, keepdims=True)
        mem_out = jnp.where(head, _dot(p.astype(BF16), mem_v), mem_out)

    merged = jnp.concatenate([mix_ref[...], mem_out.astype(BF16)], axis=1)
    h1 = h_ref[...] + _dot(merged, wo_ref[...])

    xn = _rms_norm_rows(h1, g2_ref[...]).astype(BF16)
    mlp = jnp.zeros(h1.shape, F32)
    for c0 in range(0, D_FF, FF_TILE):
        a = jnp.maximum(_dot(xn, w1_ref[:, c0:c0 + FF_TILE]), 0.0)
        mlp = mlp + _dot((a * a).astype(BF16), w2_ref[c0:c0 + FF_TILE, :])
    h2 = h1 + mlp
    if final_norm:
        h2 = _rms_norm_rows(h2, gf_ref[...])
    o_ref[...] = h2


def _post(h_rows, mix, qm, mkv_l, wo, g2, w1, w2, gf, batch, final_norm):
    rows = h_rows.shape[0]
    tiles_per_seq = rows // batch // ROW_TILE
    n_mem = mkv_l.shape[1]
    row_spec = lambda wd: pl.BlockSpec((ROW_TILE, wd), lambda i: (i, 0))
    single = pl.Buffered(1)
    const = lambda shape: pl.BlockSpec(shape, lambda i: (0,) * len(shape), pipeline_mode=single)
    return pl.pallas_call(
        functools.partial(_post_kernel, final_norm=final_norm),
        grid=(rows // ROW_TILE,),
        in_specs=[
            row_spec(D_MODEL),
            row_spec(MIX_WIDTH),
            row_spec(MEM_WIDTH),
            pl.BlockSpec((1, n_mem, 2 * MEM_WIDTH), lambda i: (i // tiles_per_seq, 0, 0)),
            const(wo.shape),
            const((1, D_MODEL)),
            const(w1.shape),
            const(w2.shape),
            const((1, D_MODEL)),
        ],
        out_specs=row_spec(D_MODEL),
        out_shape=jax.ShapeDtypeStruct(h_rows.shape, F32),
        compiler_params=pltpu.CompilerParams(
            dimension_semantics=("arbitrary",), vmem_limit_bytes=VMEM_LIMIT),
    )(h_rows, mix, qm, mkv_l, wo, g2.reshape(1, D_MODEL), w1, w2, gf.reshape(1, D_MODEL))


def kernel(x, mem, norm1_g, w_in_a, w_in_b, w_mem_kv, mem_norm_g, w_o, norm2_g, w_mlp1, w_mlp2,
           kv_norm_g, w_kv_shared, b_f, final_norm_g):
    batch, seq_len, _ = x.shape
    n_mem = mem.shape[1]
    depth = norm1_g.shape[0]
    n_a = w_in_a.shape[0]
    assert seq_len % ROW_TILE == 0 and seq_len % TQ == 0 and TQ % TK == 0
    scale = HEAD_DIM ** -0.5

    q_scale_a = jnp.concatenate([jnp.full((MIX_WIDTH,), scale, F32), jnp.ones((2 * MIX_WIDTH,), F32),
                                 jnp.full((MEM_WIDTH,), scale, F32)])
    w_a = (w_in_a * q_scale_a).astype(BF16)
    w_b = (w_in_b * scale).astype(BF16)
    w_mkv = w_mem_kv.astype(BF16)
    w_out = w_o.astype(BF16)
    w1 = w_mlp1.astype(BF16)
    w2 = w_mlp2.astype(BF16)
    pad = LANES - N_MIX_HEADS
    w_kv = jnp.pad(w_kv_shared, ((0, 0), (0, pad))).astype(BF16)
    bf_pad = jnp.pad(b_f.astype(F32), (0, pad)).reshape(1, LANES)

    h = x.reshape(batch * seq_len, D_MODEL)
    mkv = _mem_kv(mem.reshape(batch * n_mem, D_MODEL), mem_norm_g, w_mkv)
    mkv = mkv.reshape(depth * batch, n_mem, 2 * MEM_WIDTH)

    k_sh = v_sh = cq = ck = None
    for l in range(depth):
        if l == n_a:
            k_sh, v_sh, cq, ck = _shared_kv(h, kv_norm_g, w_kv, bf_pad, seq_len)
        if l < n_a:
            q, k, v, qm = _norm_proj(h, norm1_g[l], w_a[l],
                                     (MIX_WIDTH, MIX_WIDTH, MIX_WIDTH, MEM_WIDTH),
                                     scales=(LOG2E, None, None, None))
            mix = _stick_attention(q, k, v, batch, seq_len)
        else:
            q, qm = _norm_proj(h, norm1_g[l], w_b[l - n_a], (MIX_WIDTH, MEM_WIDTH))
            mix = _fox_attention(q, cq, k_sh, ck, v_sh, batch, seq_len)
        h = _post(h, mix, qm, mkv[l * batch:(l + 1) * batch], w_out[l], norm2_g[l], w1[l], w2[l],
                  final_norm_g, batch, final_norm=(l == depth - 1))
    return h.reshape(batch, seq_len, D_MODEL)
```

```python
import functools

import jax
import jax.numpy as jnp
from jax import lax
from jax.experimental import pallas as pl
from jax.experimental.pallas import tpu as pltpu

D_MODEL = 1024
HEAD_DIM = 64
N_MIX_HEADS = 8
N_MEM_HEADS = 4
MIX_WIDTH = N_MIX_HEADS * HEAD_DIM
MEM_WIDTH = N_MEM_HEADS * HEAD_DIM
D_FF = 4 * D_MODEL
EPS = 1e-6
NEG_INF = -1e30
LOG2E = 1.4426950408889634
STICK_SPENT_LOG2 = 160.0

LANES = 128
HEADS_PER_STEP = LANES // HEAD_DIM
N_PAIRS = N_MIX_HEADS // HEADS_PER_STEP
GATE_LANES = 8
N_PIECES = 3

ROW_TILE = 512
TQ = 512
TK = 256
FF_TILE = 1024
VMEM_LIMIT = 56 * 1024 * 1024

BF16 = jnp.bfloat16
F32 = jnp.float32


def _dot(a, b):
    return jnp.dot(a, b, preferred_element_type=F32)


def _dot_nt(a, b):
    return lax.dot_general(a, b, (((1,), (1,)), ((), ())), preferred_element_type=F32)


def _rms_norm_rows(x, g):
    return x * lax.rsqrt(jnp.mean(x * x, axis=-1, keepdims=True) + EPS) * g


def _const_spec(shape):
    return pl.BlockSpec(shape, lambda *_: (0,) * len(shape))


def _slot_pair(shape, dtype):
    return (pltpu.VMEM(shape, dtype), pltpu.VMEM(shape, dtype))


def _norm_proj_kernel(x_ref, g_ref, w_ref, *o_refs, scales):
    xn = _rms_norm_rows(x_ref[...], g_ref[...]).astype(BF16)
    y = _dot(xn, w_ref[...])
    off = 0
    for o_ref, scale in zip(o_refs, scales):
        width = o_ref.shape[-1]
        part = y[:, off:off + width]
        o_ref[...] = (part if scale is None else part * scale).astype(o_ref.dtype)
        off += width


def _norm_proj(x, g, w, widths, scales=None):
    rows = x.shape[0]
    n = w.shape[1]
    assert sum(widths) == n and rows % ROW_TILE == 0
    scales = (None,) * len(widths) if scales is None else scales
    return pl.pallas_call(
        functools.partial(_norm_proj_kernel, scales=scales),
        grid=(rows // ROW_TILE,),
        in_specs=[
            pl.BlockSpec((ROW_TILE, D_MODEL), lambda i: (i, 0)),
            _const_spec((1, D_MODEL)),
            _const_spec((D_MODEL, n)),
        ],
        out_specs=[pl.BlockSpec((ROW_TILE, wd), lambda i: (i, 0)) for wd in widths],
        out_shape=[jax.ShapeDtypeStruct((rows, wd), BF16) for wd in widths],
        compiler_params=pltpu.CompilerParams(
            dimension_semantics=("arbitrary",), vmem_limit_bytes=VMEM_LIMIT),
    )(x, g.reshape(1, D_MODEL), w)


def _mem_kv_kernel(x_ref, g_ref, w_ref, o_ref):
    xn = _rms_norm_rows(x_ref[...], g_ref[0]).astype(BF16)
    o_ref[0] = _dot(xn, w_ref[0]).astype(o_ref.dtype)


def _mem_kv(mem_rows, gains, w):
    depth = w.shape[0]
    rows = mem_rows.shape[0]
    return pl.pallas_call(
        _mem_kv_kernel,
        grid=(depth,),
        in_specs=[
            _const_spec((rows, D_MODEL)),
            pl.BlockSpec((1, 1, D_MODEL), lambda l: (l, 0, 0)),
            pl.BlockSpec((1, D_MODEL, 2 * MEM_WIDTH), lambda l: (l, 0, 0)),
        ],
        out_specs=pl.BlockSpec((1, rows, 2 * MEM_WIDTH), lambda l: (l, 0, 0)),
        out_shape=jax.ShapeDtypeStruct((depth, rows, 2 * MEM_WIDTH), BF16),
        compiler_params=pltpu.CompilerParams(
            dimension_semantics=("arbitrary",), vmem_limit_bytes=VMEM_LIMIT),
    )(mem_rows, gains.reshape(depth, 1, D_MODEL), w)


def _split3(c):
    hi = c.astype(BF16)
    r1 = c - hi.astype(F32)
    mid = r1.astype(BF16)
    lo = (r1 - mid.astype(F32)).astype(BF16)
    return hi, mid, lo


def _shared_kv_kernel(x_ref, g_ref, w_ref, bf_ref, eq_ref, ek_ref, oneq_ref, onek_ref,
                      k_ref, v_ref, cq_ref, ck_ref, carry_ref, *, tiles_per_seq):
    i = pl.program_id(0)

    @pl.when(i % tiles_per_seq == 0)
    def _():
        carry_ref[...] = jnp.zeros_like(carry_ref)

    xn = _rms_norm_rows(x_ref[...], g_ref[...]).astype(BF16)
    y = _dot(xn, w_ref[...])
    k_ref[...] = y[:, :MIX_WIDTH].astype(BF16)
    v_ref[...] = y[:, MIX_WIDTH:2 * MIX_WIDTH].astype(BF16)

    f = y[:, 2 * MIX_WIDTH:] + bf_ref[...]
    log_f = jnp.minimum(f, 0.0) - jnp.log1p(jnp.exp(-jnp.abs(f)))
    lane = lax.broadcasted_iota(jnp.int32, log_f.shape, 1)
    log_f = jnp.where(lane < N_MIX_HEADS, log_f, 0.0)

    rows = log_f.shape[0]
    r = lax.broadcasted_iota(jnp.int32, (rows, rows), 0)
    c = lax.broadcasted_iota(jnp.int32, (rows, rows), 1)
    tri = (c <= r).astype(BF16)
    hi, mid, lo = _split3(log_f)
    cum = (_dot(tri, hi) + _dot(tri, mid)) + _dot(tri, lo) + carry_ref[...]
    carry_ref[...] = cum[rows - 1:rows, :]

    pieces = jnp.concatenate(_split3(cum), axis=1)
    cq_ref[...] = (_dot(pieces, eq_ref[...]) + oneq_ref[...]).astype(BF16)
    ck_ref[...] = (onek_ref[...] - _dot(pieces, ek_ref[...])).astype(BF16)


def _gate_scatter_constants():
    eq = jnp.zeros((N_PIECES * LANES, LANES), F32)
    ek = jnp.zeros((N_PIECES * LANES, LANES), F32)
    oneq = jnp.zeros((1, LANES), F32)
    onek = jnp.zeros((1, LANES), F32)
    for h in range(N_MIX_HEADS):
        for p in range(N_PIECES):
            eq = eq.at[p * LANES + h, GATE_LANES * h + p].set(1.0)
            ek = ek.at[p * LANES + h, GATE_LANES * h + N_PIECES + p].set(1.0)
            oneq = oneq.at[0, GATE_LANES * h + N_PIECES + p].set(1.0)
            onek = onek.at[0, GATE_LANES * h + p].set(1.0)
    return eq.astype(BF16), ek.astype(BF16), oneq, onek


def _shared_kv(h_rows, g, w_pad, bf_pad, seq_len):
    rows = h_rows.shape[0]
    n = w_pad.shape[1]
    eq, ek, oneq, onek = _gate_scatter_constants()
    kern = functools.partial(_shared_kv_kernel, tiles_per_seq=seq_len // ROW_TILE)
    row_spec = lambda wd: pl.BlockSpec((ROW_TILE, wd), lambda i: (i, 0))
    return pl.pallas_call(
        kern,
        grid=(rows // ROW_TILE,),
        in_specs=[
            row_spec(D_MODEL),
            _const_spec((1, D_MODEL)),
            _const_spec((D_MODEL, n)),
            _const_spec((1, LANES)),
            _const_spec(eq.shape),
            _const_spec(ek.shape),
            _const_spec((1, LANES)),
            _const_spec((1, LANES)),
        ],
        out_specs=[row_spec(MIX_WIDTH), row_spec(MIX_WIDTH), row_spec(LANES), row_spec(LANES)],
        out_shape=[
            jax.ShapeDtypeStruct((rows, MIX_WIDTH), BF16),
            jax.ShapeDtypeStruct((rows, MIX_WIDTH), BF16),
            jax.ShapeDtypeStruct((rows, LANES), BF16),
            jax.ShapeDtypeStruct((rows, LANES), BF16),
        ],
        scratch_shapes=[pltpu.VMEM((1, LANES), F32)],
        compiler_params=pltpu.CompilerParams(
            dimension_semantics=("arbitrary",), vmem_limit_bytes=VMEM_LIMIT),
    )(h_rows, g.reshape(1, D_MODEL), w_pad, bf_pad, eq, ek, oneq, onek)


def _head_lane_masks():
    lane = lax.broadcasted_iota(jnp.int32, (1, LANES), 1)
    return [(lane >= h * HEAD_DIM) & (lane < (h + 1) * HEAD_DIM) for h in range(HEADS_PER_STEP)]


def _stack_heads(x2, masks):
    zero = jnp.zeros_like(x2)
    return jnp.concatenate([jnp.where(m, x2, zero) for m in masks], axis=0)


def _heads_to_lanes(x):
    n = x.shape[0] // HEADS_PER_STEP
    return jnp.concatenate([x[h * n:(h + 1) * n] for h in range(HEADS_PER_STEP)], axis=1)


def _visible(t0, start, width, strict):
    t_pos = t0 + (lax.broadcasted_iota(jnp.int32, (HEADS_PER_STEP * TQ, width), 0) & (TQ - 1))
    s_pos = start + lax.broadcasted_iota(jnp.int32, (HEADS_PER_STEP * TQ, width), 1)
    return s_pos < t_pos if strict else s_pos <= t_pos


def _stick_kernel(q_ref, k_ref, v_ref, o_ref, q_scr, tri_scr, z_scr, zl_scr, w_scr, later_scr, acc_scr):
    t0 = pl.program_id(2) * TQ
    masks = _head_lane_masks()
    q_scr[...] = _stack_heads(q_ref[...], masks)
    r = lax.broadcasted_iota(jnp.int32, (TK, TK), 0)
    c = lax.broadcasted_iota(jnp.int32, (TK, TK), 1)
    tri_scr[...] = (r >= c).astype(BF16)
    later_scr[...] = jnp.zeros_like(later_scr)
    acc_scr[...] = jnp.zeros_like(acc_scr)

    def key_start(n):
        return pl.multiple_of(jnp.maximum(t0 + (TQ // TK - 1 - n) * TK, 0), TK)

    def visible(n):
        return _visible(t0, t0 + (TQ // TK - 1 - n) * TK, TK, strict=True)

    def scores(n, slot):
        z_scr[slot][...] = _dot_nt(q_scr[...], k_ref[pl.ds(key_start(n), TK), :])

    def weights(slot, mask=None):
        z = z_scr[slot][...]
        neg_abs = lax.bitcast_convert_type(
            lax.bitcast_convert_type(z, jnp.uint32) | jnp.uint32(0x80000000), F32)
        sp = jnp.maximum(z, 0.0) + jnp.log(1.0 + jnp.exp2(neg_abs)) * LOG2E
        if mask is not None:
            sp = jnp.where(mask, sp, 0.0)
        zl_scr[...] = z - later_scr[...]
        within = _dot(sp.astype(BF16), tri_scr[...])
        w = jnp.exp2(zl_scr[...] - within)
        if mask is not None:
            w = jnp.where(mask, w, 0.0)
        w_scr[slot][...] = _heads_to_lanes(w.astype(BF16))
        later_scr[...] += within[:, 0:1]

    def values(n, slot):
        acc_scr[...] += _dot(w_scr[slot][...], _stack_heads(v_ref[pl.ds(key_start(n), TK), :], masks))

    assert TQ == 2 * TK
    n_blocks = TQ // TK + t0 // TK
    scores(0, 0)
    weights(0, visible(0)); scores(1, 1)
    values(0, 0); weights(1, visible(1)); scores(2, 0)

    def more(carry):
        n, spent = carry
        return jnp.logical_and(n < n_blocks - 2, jnp.logical_not(spent))

    def trip(carry):
        n, _ = carry
        scores(n + 2, 1); values(n, 1); weights(0)
        scores(n + 3, 0); values(n + 1, 0); weights(1)
        return n + 2, jnp.min(later_scr[...]) >= STICK_SPENT_LOG2

    n_end, _ = lax.while_loop(more, trip, (jnp.int32(1), jnp.bool_(False)))
    values(n_end, 1)
    o_ref[...] = acc_scr[...].astype(o_ref.dtype)


def _stick_attention(q, k, v, batch, seq_len):
    n_q = seq_len // TQ
    q_spec = pl.BlockSpec((TQ, LANES), lambda b, p, i: (b * n_q + i, p))
    kv_spec = pl.BlockSpec((seq_len, LANES), lambda b, p, i: (b, p))
    return pl.pallas_call(
        _stick_kernel,
        grid=(batch, N_PAIRS, n_q),
        in_specs=[q_spec, kv_spec, kv_spec],
        out_specs=q_spec,
        out_shape=jax.ShapeDtypeStruct(q.shape, BF16),
        scratch_shapes=[
            pltpu.VMEM((HEADS_PER_STEP * TQ, LANES), BF16),
            pltpu.VMEM((TK, TK), BF16),
            _slot_pair((HEADS_PER_STEP * TQ, TK), F32),
            pltpu.VMEM((HEADS_PER_STEP * TQ, TK), F32),
            _slot_pair((TQ, HEADS_PER_STEP * TK), BF16),
            pltpu.VMEM((HEADS_PER_STEP * TQ, 1), F32),
            pltpu.VMEM((TQ, LANES), F32),
        ],
        compiler_params=pltpu.CompilerParams(
            dimension_semantics=("arbitrary", "arbitrary", "arbitrary"),
            vmem_limit_bytes=VMEM_LIMIT),
    )(q, k, v)


def _fox_kernel(q_ref, cq_ref, k_ref, ck_ref, v_ref, o_ref,
                q_scr, ones_scr, z_scr, p_scr, alpha_scr, m_scr, acc_scr):
    pair = pl.program_id(1)
    t0 = pl.program_id(2) * TQ
    masks = _head_lane_masks()
    lane = lax.broadcasted_iota(jnp.int32, (1, LANES), 1)
    cq = cq_ref[...]
    gate_rows = []
    for h in range(HEADS_PER_STEP):
        g0 = (pair * HEADS_PER_STEP + h) * GATE_LANES
        gate_rows.append(jnp.where((lane >= g0) & (lane < g0 + GATE_LANES), cq, jnp.zeros_like(cq)))
    q_scr[...] = jnp.concatenate([_stack_heads(q_ref[...], masks), jnp.concatenate(gate_rows, axis=0)],
                                 axis=1)
    lane_rows = lax.broadcasted_iota(jnp.int32, (HEADS_PER_STEP * TK, LANES), 1) // HEAD_DIM
    head_rows = lax.broadcasted_iota(jnp.int32, (HEADS_PER_STEP * TK, LANES), 0) // TK
    ones_scr[...] = (lane_rows == head_rows).astype(F32).astype(BF16)
    m_scr[...] = jnp.full_like(m_scr, NEG_INF)
    acc_scr[...] = jnp.zeros_like(acc_scr)
    last_start = k_ref.shape[0] - TK

    def key_start(n):
        return pl.multiple_of(jnp.minimum(n * TK, last_start), TK)

    def visible(n):
        return _visible(t0, n * TK, TK, strict=False)

    def scores(n, slot):
        start = key_start(n)
        k_blk = jnp.concatenate([k_ref[pl.ds(start, TK), :], ck_ref[pl.ds(start, TK), :]], axis=1)
        z_scr[slot][...] = _dot_nt(q_scr[...], k_blk)

    def probs(slot, mask=None):
        z = z_scr[slot][...]
        if mask is not None:
            z = jnp.where(mask, z, NEG_INF)
        m = m_scr[...]
        m_new = jnp.maximum(m, jnp.max(z, axis=1, keepdims=True))
        m_scr[...] = m_new
        shift = jnp.concatenate([m_new] * (TK // LANES), axis=1)
        p_scr[slot][...] = _heads_to_lanes(jnp.exp(z - shift).astype(BF16))
        alpha = jnp.exp(m - m_new)
        alpha_scr[slot][...] = jnp.where(masks[0], alpha[:TQ], alpha[TQ:])

    def accumulate(n, slot):
        v_rows = jnp.concatenate(
            [_stack_heads(v_ref[pl.ds(key_start(n), TK), :], masks), ones_scr[...]], axis=1)
        alpha = alpha_scr[slot][...]
        acc_scr[...] = (jnp.concatenate([alpha, alpha], axis=1) * acc_scr[...]
                        + _dot(p_scr[slot][...], v_rows))

    n_blocks = TQ // TK + t0 // TK
    scores(0, 0)
    probs(0, visible(0)); scores(1, 1)
    accumulate(0, 0); probs(1, visible(1)); scores(2, 0)

    @pl.loop(1, n_blocks - 3, step=2)
    def _(n):
        scores(n + 2, 1); accumulate(n, 1); probs(0)
        scores(n + 3, 0); accumulate(n + 1, 0); probs(1)

    @pl.when(n_blocks >= 4)
    def _():
        accumulate(n_blocks - 3, 1); probs(0, visible(n_blocks - 2)); scores(n_blocks - 1, 1)
        accumulate(n_blocks - 2, 0); probs(1, visible(n_blocks - 1))

    accumulate(n_blocks - 1, 1)
    acc = acc_scr[...]
    o_ref[...] = (acc[:, :LANES] / acc[:, LANES:]).astype(o_ref.dtype)


def _fox_attention(q, cq, k, ck, v, batch, seq_len):
    n_q = seq_len // TQ
    q_spec = pl.BlockSpec((TQ, LANES), lambda b, p, i: (b * n_q + i, p))
    cq_spec = pl.BlockSpec((TQ, LANES), lambda b, p, i: (b * n_q + i, 0))
    kv_spec = pl.BlockSpec((seq_len, LANES), lambda b, p, i: (b, p))
    ck_spec = pl.BlockSpec((seq_len, LANES), lambda b, p, i: (b, 0))
    return pl.pallas_call(
        _fox_kernel,
        grid=(batch, N_PAIRS, n_q),
        in_specs=[q_spec, cq_spec, kv_spec, ck_spec, kv_spec],
        out_specs=q_spec,
        out_shape=jax.ShapeDtypeStruct(q.shape, BF16),
        scratch_shapes=[
            pltpu.VMEM((HEADS_PER_STEP * TQ, 2 * LANES), BF16),
            pltpu.VMEM((HEADS_PER_STEP * TK, LANES), BF16),
            _slot_pair((HEADS_PER_STEP * TQ, TK), F32),
            _slot_pair((TQ, HEADS_PER_STEP * TK), BF16),
            _slot_pair((TQ, LANES), F32),
            pltpu.VMEM((HEADS_PER_STEP * TQ, LANES), F32),
            pltpu.VMEM((TQ, 2 * LANES), F32),
        ],
        compiler_params=pltpu.CompilerParams(
            dimension_semantics=("arbitrary", "arbitrary", "arbitrary"),
            vmem_limit_bytes=VMEM_LIMIT),
    )(q, cq, k, ck, v)


def _post_kernel(h_ref, mix_ref, qm_ref, mkv_ref, wo_ref, g2_ref, w1_ref, w2_ref, gf_ref, o_ref,
                 *, final_norm):
    qm = qm_ref[...]
    mem_k = mkv_ref[0, :, :MEM_WIDTH]
    mem_v = mkv_ref[0, :, MEM_WIDTH:]
    lane = lax.broadcasted_iota(jnp.int32, (1, MEM_WIDTH), 1)
    zero = jnp.zeros_like(qm)
    mem_out = jnp.zeros(qm.shape, F32)
    for hd in range(N_MEM_HEADS):
        head = (lane >= hd * HEAD_DIM) & (lane < (hd + 1) * HEAD_DIM)
        s = _dot_nt(jnp.where(head, qm, zero), mem_k)
        p = jnp.exp(s - jnp.max(s, axis=1, keepdims=True))
        p = p / jnp.sum(p, axis=1, keepdims=True)
        mem_out = jnp.where(head, _dot(p.astype(BF16), mem_v), mem_out)

    merged = jnp.concatenate([mix_ref[...], mem_out.astype(BF16)], axis=1)
    h1 = h_ref[...] + _dot(merged, wo_ref[...])

    xn = _rms_norm_rows(h1, g2_ref[...]).astype(BF16)
    mlp = jnp.zeros(h1.shape, F32)
    for c0 in range(0, D_FF, FF_TILE):
        a = jnp.maximum(_dot(xn, w1_ref[:, c0:c0 + FF_TILE]), 0.0)
        mlp = mlp + _dot((a * a).astype(BF16), w2_ref[c0:c0 + FF_TILE, :])
    h2 = h1 + mlp
    if final_norm:
        h2 = _rms_norm_rows(h2, gf_ref[...])
    o_ref[...] = h2


def _post(h_rows, mix, qm, mkv_l, wo, g2, w1, w2, gf, batch, final_norm):
    rows = h_rows.shape[0]
    tiles_per_seq = rows // batch // ROW_TILE
    n_mem = mkv_l.shape[1]
    row_spec = lambda wd: pl.BlockSpec((ROW_TILE, wd), lambda i: (i, 0))
    single = pl.Buffered(1)
    const = lambda shape: pl.BlockSpec(shape, lambda i: (0,) * len(shape), pipeline_mode=single)
    return pl.pallas_call(
        functools.partial(_post_kernel, final_norm=final_norm),
        grid=(rows // ROW_TILE,),
        in_specs=[
            row_spec(D_MODEL),
            row_spec(MIX_WIDTH),
            row_spec(MEM_WIDTH),
            pl.BlockSpec((1, n_mem, 2 * MEM_WIDTH), lambda i: (i // tiles_per_seq, 0, 0)),
            const(wo.shape),
            const((1, D_MODEL)),
            const(w1.shape),
            const(w2.shape),
            const((1, D_MODEL)),
        ],
        out_specs=row_spec(D_MODEL),
        out_shape=jax.ShapeDtypeStruct(h_rows.shape, F32),
        compiler_params=pltpu.CompilerParams(
            dimension_semantics=("arbitrary",), vmem_limit_bytes=VMEM_LIMIT),
    )(h_rows, mix, qm, mkv_l, wo, g2.reshape(1, D_MODEL), w1, w2, gf.reshape(1, D_MODEL))


def kernel(x, mem, norm1_g, w_in_a, w_in_b, w_mem_kv, mem_norm_g, w_o, norm2_g, w_mlp1, w_mlp2,
           kv_norm_g, w_kv_shared, b_f, final_norm_g):
    batch, seq_len, _ = x.shape
    n_mem = mem.shape[1]
    depth = norm1_g.shape[0]
    n_a = w_in_a.shape[0]
    assert seq_len % ROW_TILE == 0 and seq_len % TQ == 0 and TQ % TK == 0
    scale = HEAD_DIM ** -0.5

    q_scale_a = jnp.concatenate([jnp.full((MIX_WIDTH,), scale, F32), jnp.ones((2 * MIX_WIDTH,), F32),
                                 jnp.full((MEM_WIDTH,), scale, F32)])
    w_a = (w_in_a * q_scale_a).astype(BF16)
    w_b = (w_in_b * scale).astype(BF16)
    w_mkv = w_mem_kv.astype(BF16)
    w_out = w_o.astype(BF16)
    w1 = w_mlp1.astype(BF16)
    w2 = w_mlp2.astype(BF16)
    pad = LANES - N_MIX_HEADS
    w_kv = jnp.pad(w_kv_shared, ((0, 0), (0, pad))).astype(BF16)
    bf_pad = jnp.pad(b_f.astype(F32), (0, pad)).reshape(1, LANES)

    h = x.reshape(batch * seq_len, D_MODEL)
    mkv = _mem_kv(mem.reshape(batch * n_mem, D_MODEL), mem_norm_g, w_mkv)
    mkv = mkv.reshape(depth * batch, n_mem, 2 * MEM_WIDTH)

    k_sh = v_sh = cq = ck = None
    for l in range(depth):
        if l == n_a:
            k_sh, v_sh, cq, ck = _shared_kv(h, kv_norm_g, w_kv, bf_pad, seq_len)
        if l < n_a:
            q, k, v, qm = _norm_proj(h, norm1_g[l], w_a[l],
                                     (MIX_WIDTH, MIX_WIDTH, MIX_WIDTH, MEM_WIDTH),
                                     scales=(LOG2E, None, None, None))
            mix = _stick_attention(q, k, v, batch, seq_len)
        else:
            q, qm = _norm_proj(h, norm1_g[l], w_b[l - n_a], (MIX_WIDTH, MEM_WIDTH))
            mix = _fox_attention(q, cq, k_sh, ck, v_sh, batch, seq_len)
        h = _post(h, mix, qm, mkv[l * batch:(l + 1) * batch], w_out[l], norm2_g[l], w1[l], w2[l],
                  final_norm_g, batch, final_norm=(l == depth - 1))
    return h.reshape(batch, seq_len, D_MODEL)
```

```python
import functools

import jax
import jax.numpy as jnp
from jax import lax
from jax.experimental import pallas as pl
from jax.experimental.pallas import tpu as pltpu

D_MODEL = 1024
HEAD_DIM = 64
N_MIX_HEADS = 8
N_MEM_HEADS = 4
MIX_WIDTH = N_MIX_HEADS * HEAD_DIM
MEM_WIDTH = N_MEM_HEADS * HEAD_DIM
D_FF = 4 * D_MODEL
EPS = 1e-6
NEG_INF = -1e30
LOG2E = 1.4426950408889634
STICK_SPENT_LOG2 = 160.0
FOX_NEGLIGIBLE = 110.0
NORM_MARGIN = 1.01
STATS_ROWS = 8

LANES = 128
HEADS_PER_STEP = LANES // HEAD_DIM
N_PAIRS = N_MIX_HEADS // HEADS_PER_STEP
GATE_LANES = 8
N_PIECES = 3

ROW_TILE = 512
TQ = 512
TK = 256
FF_TILE = 1024
VMEM_LIMIT = 56 * 1024 * 1024

BF16 = jnp.bfloat16
F32 = jnp.float32


def _dot(a, b):
    return jnp.dot(a, b, preferred_element_type=F32)


def _dot_nt(a, b):
    return lax.dot_general(a, b, (((1,), (1,)), ((), ())), preferred_element_type=F32)


def _rms_norm_rows(x, g):
    return x * lax.rsqrt(jnp.mean(x * x, axis=-1, keepdims=True) + EPS) * g


def _const_spec(shape):
    return pl.BlockSpec(shape, lambda *_: (0,) * len(shape))


def _slot_pair(shape, dtype):
    return (pltpu.VMEM(shape, dtype), pltpu.VMEM(shape, dtype))


def _norm_proj_kernel(x_ref, g_ref, w_ref, *o_refs, scales):
    xn = _rms_norm_rows(x_ref[...], g_ref[...]).astype(BF16)
    y = _dot(xn, w_ref[...])
    off = 0
    for o_ref, scale in zip(o_refs, scales):
        width = o_ref.shape[-1]
        part = y[:, off:off + width]
        o_ref[...] = (part if scale is None else part * scale).astype(o_ref.dtype)
        off += width


def _norm_proj(x, g, w, widths, scales=None):
    rows = x.shape[0]
    n = w.shape[1]
    assert sum(widths) == n and rows % ROW_TILE == 0
    scales = (None,) * len(widths) if scales is None else scales
    return pl.pallas_call(
        functools.partial(_norm_proj_kernel, scales=scales),
        grid=(rows // ROW_TILE,),
        in_specs=[
            pl.BlockSpec((ROW_TILE, D_MODEL), lambda i: (i, 0)),
            _const_spec((1, D_MODEL)),
            _const_spec((D_MODEL, n)),
        ],
        out_specs=[pl.BlockSpec((ROW_TILE, wd), lambda i: (i, 0)) for wd in widths],
        out_shape=[jax.ShapeDtypeStruct((rows, wd), BF16) for wd in widths],
        compiler_params=pltpu.CompilerParams(
            dimension_semantics=("arbitrary",), vmem_limit_bytes=VMEM_LIMIT),
    )(x, g.reshape(1, D_MODEL), w)


def _mem_kv_kernel(x_ref, g_ref, w_ref, o_ref):
    xn = _rms_norm_rows(x_ref[...], g_ref[0]).astype(BF16)
    o_ref[0] = _dot(xn, w_ref[0]).astype(o_ref.dtype)


def _mem_kv(mem_rows, gains, w):
    depth = w.shape[0]
    rows = mem_rows.shape[0]
    return pl.pallas_call(
        _mem_kv_kernel,
        grid=(depth,),
        in_specs=[
            _const_spec((rows, D_MODEL)),
            pl.BlockSpec((1, 1, D_MODEL), lambda l: (l, 0, 0)),
            pl.BlockSpec((1, D_MODEL, 2 * MEM_WIDTH), lambda l: (l, 0, 0)),
        ],
        out_specs=pl.BlockSpec((1, rows, 2 * MEM_WIDTH), lambda l: (l, 0, 0)),
        out_shape=jax.ShapeDtypeStruct((depth, rows, 2 * MEM_WIDTH), BF16),
        compiler_params=pltpu.CompilerParams(
            dimension_semantics=("arbitrary",), vmem_limit_bytes=VMEM_LIMIT),
    )(mem_rows, gains.reshape(depth, 1, D_MODEL), w)


def _split3(c):
    hi = c.astype(BF16)
    r1 = c - hi.astype(F32)
    mid = r1.astype(BF16)
    lo = (r1 - mid.astype(F32)).astype(BF16)
    return hi, mid, lo


def _shared_kv_kernel(x_ref, g_ref, w_ref, bf_ref, eq_ref, ek_ref, oneq_ref, onek_ref, ehead_ref,
                      k_ref, v_ref, cq_ref, ck_ref, stats_ref, carry_ref, knorm_ref, *, tiles_per_seq):
    i = pl.program_id(0)

    @pl.when(i % tiles_per_seq == 0)
    def _():
        carry_ref[...] = jnp.zeros_like(carry_ref)
        knorm_ref[...] = jnp.zeros_like(knorm_ref)

    xn = _rms_norm_rows(x_ref[...], g_ref[...]).astype(BF16)
    y = _dot(xn, w_ref[...])
    k_bf = y[:, :MIX_WIDTH].astype(BF16)
    k_ref[...] = k_bf
    v_ref[...] = y[:, MIX_WIDTH:2 * MIX_WIDTH].astype(BF16)

    k_f = k_bf.astype(F32)
    k_sq = _dot((k_f * k_f).astype(BF16), ehead_ref[...])
    knorm_ref[...] = jnp.maximum(knorm_ref[...], jnp.max(k_sq, axis=0, keepdims=True))

    f = y[:, 2 * MIX_WIDTH:] + bf_ref[...]
    log_f = jnp.minimum(f, 0.0) - jnp.log1p(jnp.exp(-jnp.abs(f)))
    lane = lax.broadcasted_iota(jnp.int32, log_f.shape, 1)
    log_f = jnp.where(lane < N_MIX_HEADS, log_f, 0.0)

    rows = log_f.shape[0]
    r = lax.broadcasted_iota(jnp.int32, (rows, rows), 0)
    c = lax.broadcasted_iota(jnp.int32, (rows, rows), 1)
    tri = (c <= r).astype(BF16)
    hi, mid, lo = _split3(log_f)
    cum = (_dot(tri, hi) + _dot(tri, mid)) + _dot(tri, lo) + carry_ref[...]
    carry_ref[...] = cum[rows - 1:rows, :]

    stats_ref[...] = jnp.concatenate(
        [cum[rows - 1:rows, :], jnp.sqrt(knorm_ref[...]) * NORM_MARGIN,
         jnp.zeros((STATS_ROWS - 2, LANES), F32)], axis=0)

    pieces = jnp.concatenate(_split3(cum), axis=1)
    cq_ref[...] = (_dot(pieces, eq_ref[...]) + oneq_ref[...]).astype(BF16)
    ck_ref[...] = (onek_ref[...] - _dot(pieces, ek_ref[...])).astype(BF16)


def _gate_scatter_constants():
    eq = jnp.zeros((N_PIECES * LANES, LANES), F32)
    ek = jnp.zeros((N_PIECES * LANES, LANES), F32)
    oneq = jnp.zeros((1, LANES), F32)
    onek = jnp.zeros((1, LANES), F32)
    for h in range(N_MIX_HEADS):
        for p in range(N_PIECES):
            eq = eq.at[p * LANES + h, GATE_LANES * h + p].set(1.0)
            ek = ek.at[p * LANES + h, GATE_LANES * h + N_PIECES + p].set(1.0)
            oneq = oneq.at[0, GATE_LANES * h + N_PIECES + p].set(1.0)
            onek = onek.at[0, GATE_LANES * h + p].set(1.0)
    return eq.astype(BF16), ek.astype(BF16), oneq, onek


def _head_indicator():
    lane = jnp.arange(MIX_WIDTH)[:, None] // HEAD_DIM
    return (lane == jnp.arange(LANES)[None, :]).astype(BF16)


def _shared_kv(h_rows, g, w_pad, bf_pad, seq_len):
    rows = h_rows.shape[0]
    n = w_pad.shape[1]
    eq, ek, oneq, onek = _gate_scatter_constants()
    ehead = _head_indicator()
    kern = functools.partial(_shared_kv_kernel, tiles_per_seq=seq_len // ROW_TILE)
    row_spec = lambda wd: pl.BlockSpec((ROW_TILE, wd), lambda i: (i, 0))
    return pl.pallas_call(
        kern,
        grid=(rows // ROW_TILE,),
        in_specs=[
            row_spec(D_MODEL),
            _const_spec((1, D_MODEL)),
            _const_spec((D_MODEL, n)),
            _const_spec((1, LANES)),
            _const_spec(eq.shape),
            _const_spec(ek.shape),
            _const_spec((1, LANES)),
            _const_spec((1, LANES)),
            _const_spec(ehead.shape),
        ],
        out_specs=[row_spec(MIX_WIDTH), row_spec(MIX_WIDTH), row_spec(LANES), row_spec(LANES),
                   pl.BlockSpec((STATS_ROWS, LANES), lambda i: (i, 0))],
        out_shape=[
            jax.ShapeDtypeStruct((rows, MIX_WIDTH), BF16),
            jax.ShapeDtypeStruct((rows, MIX_WIDTH), BF16),
            jax.ShapeDtypeStruct((rows, LANES), BF16),
            jax.ShapeDtypeStruct((rows, LANES), BF16),
            jax.ShapeDtypeStruct((rows // ROW_TILE * STATS_ROWS, LANES), F32),
        ],
        scratch_shapes=[pltpu.VMEM((1, LANES), F32), pltpu.VMEM((1, LANES), F32)],
        compiler_params=pltpu.CompilerParams(
            dimension_semantics=("arbitrary",), vmem_limit_bytes=VMEM_LIMIT),
    )(h_rows, g.reshape(1, D_MODEL), w_pad, bf_pad, eq, ek, oneq, onek, ehead)


def _head_lane_masks():
    lane = lax.broadcasted_iota(jnp.int32, (1, LANES), 1)
    return [(lane >= h * HEAD_DIM) & (lane < (h + 1) * HEAD_DIM) for h in range(HEADS_PER_STEP)]


def _stack_heads(x2, masks):
    zero = jnp.zeros_like(x2)
    return jnp.concatenate([jnp.where(m, x2, zero) for m in masks], axis=0)


def _heads_to_lanes(x):
    n = x.shape[0] // HEADS_PER_STEP
    return jnp.concatenate([x[h * n:(h + 1) * n] for h in range(HEADS_PER_STEP)], axis=1)


def _visible(t0, start, width, strict):
    t_pos = t0 + (lax.broadcasted_iota(jnp.int32, (HEADS_PER_STEP * TQ, width), 0) & (TQ - 1))
    s_pos = start + lax.broadcasted_iota(jnp.int32, (HEADS_PER_STEP * TQ, width), 1)
    return s_pos < t_pos if strict else s_pos <= t_pos


def _stick_kernel(q_ref, k_ref, v_ref, o_ref, q_scr, tri_scr, z_scr, zl_scr, w_scr, later_scr, acc_scr):
    t0 = pl.program_id(2) * TQ
    masks = _head_lane_masks()
    q_scr[...] = _stack_heads(q_ref[...], masks)
    r = lax.broadcasted_iota(jnp.int32, (TK, TK), 0)
    c = lax.broadcasted_iota(jnp.int32, (TK, TK), 1)
    tri_scr[...] = (r >= c).astype(BF16)
    later_scr[...] = jnp.zeros_like(later_scr)
    acc_scr[...] = jnp.zeros_like(acc_scr)

    def key_start(n):
        return pl.multiple_of(jnp.maximum(t0 + (TQ // TK - 1 - n) * TK, 0), TK)

    def visible(n):
        return _visible(t0, t0 + (TQ // TK - 1 - n) * TK, TK, strict=True)

    def scores(n, slot):
        z_scr[slot][...] = _dot_nt(q_scr[...], k_ref[pl.ds(key_start(n), TK), :])

    def weights(slot, mask=None):
        z = z_scr[slot][...]
        neg_abs = lax.bitcast_convert_type(
            lax.bitcast_convert_type(z, jnp.uint32) | jnp.uint32(0x80000000), F32)
        sp = jnp.maximum(z, 0.0) + jnp.log(1.0 + jnp.exp2(neg_abs)) * LOG2E
        if mask is not None:
            sp = jnp.where(mask, sp, 0.0)
        zl_scr[...] = z - later_scr[...]
        within = _dot(sp.astype(BF16), tri_scr[...])
        w = jnp.exp2(zl_scr[...] - within)
        if mask is not None:
            w = jnp.where(mask, w, 0.0)
        w_scr[slot][...] = _heads_to_lanes(w.astype(BF16))
        later_scr[...] += within[:, 0:1]

    def values(n, slot):
        acc_scr[...] += _dot(w_scr[slot][...], _stack_heads(v_ref[pl.ds(key_start(n), TK), :], masks))

    assert TQ == 2 * TK
    n_blocks = TQ // TK + t0 // TK
    scores(0, 0)
    weights(0, visible(0)); scores(1, 1)
    values(0, 0); weights(1, visible(1)); scores(2, 0)

    def more(carry):
        n, spent = carry
        return jnp.logical_and(n < n_blocks - 2, jnp.logical_not(spent))

    def trip(carry):
        n, _ = carry
        scores(n + 2, 1); values(n, 1); weights(0)
        scores(n + 3, 0); values(n + 1, 0); weights(1)
        return n + 2, jnp.min(later_scr[...]) >= STICK_SPENT_LOG2

    n_end, _ = lax.while_loop(more, trip, (jnp.int32(1), jnp.bool_(False)))
    values(n_end, 1)
    o_ref[...] = acc_scr[...].astype(o_ref.dtype)


def _stick_attention(q, k, v, batch, seq_len):
    n_q = seq_len // TQ
    q_spec = pl.BlockSpec((TQ, LANES), lambda b, p, i: (b * n_q + i, p))
    kv_spec = pl.BlockSpec((seq_len, LANES), lambda b, p, i: (b, p))
    return pl.pallas_call(
        _stick_kernel,
        grid=(batch, N_PAIRS, n_q),
        in_specs=[q_spec, kv_spec, kv_spec],
        out_specs=q_spec,
        out_shape=jax.ShapeDtypeStruct(q.shape, BF16),
        scratch_shapes=[
            pltpu.VMEM((HEADS_PER_STEP * TQ, LANES), BF16),
            pltpu.VMEM((TK, TK), BF16),
            _slot_pair((HEADS_PER_STEP * TQ, TK), F32),
            pltpu.VMEM((HEADS_PER_STEP * TQ, TK), F32),
            _slot_pair((TQ, HEADS_PER_STEP * TK), BF16),
            pltpu.VMEM((HEADS_PER_STEP * TQ, 1), F32),
            pltpu.VMEM((TQ, LANES), F32),
        ],
        compiler_params=pltpu.CompilerParams(
            dimension_semantics=("arbitrary", "arbitrary", "arbitrary"),
            vmem_limit_bytes=VMEM_LIMIT),
    )(q, k, v)


def _fox_kernel(stats_ref, q_ref, cq_ref, k_ref, ck_ref, v_ref, o_ref,
                q_scr, ones_scr, z_scr, p_scr, alpha_scr, m_scr, acc_scr, qnorm_scr, crow_scr):
    batch_idx = pl.program_id(0)
    pair = pl.program_id(1)
    t0 = pl.program_id(2) * TQ
    masks = _head_lane_masks()
    lane = lax.broadcasted_iota(jnp.int32, (1, LANES), 1)
    cq = cq_ref[...]
    gate_rows = []
    for h in range(HEADS_PER_STEP):
        g0 = (pair * HEADS_PER_STEP + h) * GATE_LANES
        gate_rows.append(jnp.where((lane >= g0) & (lane < g0 + GATE_LANES), cq, jnp.zeros_like(cq)))
    q_rows = _stack_heads(q_ref[...], masks)
    gates = jnp.concatenate(gate_rows, axis=0)
    q_scr[...] = jnp.concatenate([q_rows, gates], axis=1)
    lane_rows = lax.broadcasted_iota(jnp.int32, (HEADS_PER_STEP * TK, LANES), 1) // HEAD_DIM
    head_rows = lax.broadcasted_iota(jnp.int32, (HEADS_PER_STEP * TK, LANES), 0) // TK
    ones_scr[...] = (lane_rows == head_rows).astype(F32).astype(BF16)
    m_scr[...] = jnp.full_like(m_scr, NEG_INF)
    acc_scr[...] = jnp.zeros_like(acc_scr)

    q_f = q_rows.astype(F32)
    q_norm = jnp.sqrt(jnp.sum(q_f * q_f, axis=1, keepdims=True)) * NORM_MARGIN
    qnorm_scr[...] = jnp.broadcast_to(q_norm, qnorm_scr.shape)
    piece_lanes = (lane % GATE_LANES) < N_PIECES
    c_row = jnp.sum(jnp.where(piece_lanes, gates.astype(F32), 0.0), axis=1, keepdims=True)
    crow_scr[...] = jnp.broadcast_to(c_row, crow_scr.shape)

    def key_start(n):
        back = jnp.maximum(t0 - (n - 1) * TK, 0)
        return pl.multiple_of(jnp.where(n < 2, t0 + n * TK, back), TK)

    def visible(n):
        return _visible(t0, t0 + n * TK, TK, strict=False)

    def scores(n, slot):
        start = key_start(n)
        k_blk = jnp.concatenate([k_ref[pl.ds(start, TK), :], ck_ref[pl.ds(start, TK), :]], axis=1)
        z_scr[slot][...] = _dot_nt(q_scr[...], k_blk)

    def probs(slot, mask=None):
        z = z_scr[slot][...]
        if mask is not None:
            z = jnp.where(mask, z, NEG_INF)
        m = m_scr[...]
        m_new = jnp.maximum(m, jnp.max(z, axis=1, keepdims=True))
        m_scr[...] = m_new
        shift = jnp.concatenate([m_new] * (TK // LANES), axis=1)
        p_scr[slot][...] = _heads_to_lanes(jnp.exp(z - shift).astype(BF16))
        alpha = jnp.exp(m - m_new)
        alpha_scr[slot][...] = jnp.where(masks[0], alpha[:TQ], alpha[TQ:])

    def accumulate(n, slot):
        v_rows = jnp.concatenate(
            [_stack_heads(v_ref[pl.ds(key_start(n), TK), :], masks), ones_scr[...]], axis=1)
        alpha = alpha_scr[slot][...]
        acc_scr[...] = (jnp.concatenate([alpha, alpha], axis=1) * acc_scr[...]
                        + _dot(p_scr[slot][...], v_rows))

    def rest_is_negligible(n_done):
        last_key = key_start(n_done) - 1
        tile = jnp.maximum(last_key, 0) // ROW_TILE
        worst = None
        for h in range(HEADS_PER_STEP):
            head = pair * HEADS_PER_STEP + h
            c_floor = stats_ref[batch_idx, tile, head]
            k_norm = stats_ref[batch_idx, tile, N_MIX_HEADS + head]
            rows = slice(h * TQ, (h + 1) * TQ)
            gap = (qnorm_scr[rows, :] * k_norm + (crow_scr[rows, :] - c_floor)
                   - m_scr[rows, :] + FOX_NEGLIGIBLE)
            top = jnp.max(gap)
            worst = top if worst is None else jnp.maximum(worst, top)
        return worst <= 0.0

    assert TQ == 2 * TK
    n_blocks = TQ // TK + t0 // TK
    scores(0, 0)
    probs(0, visible(0)); scores(1, 1)
    accumulate(0, 0); probs(1, visible(1)); scores(2, 0)

    def more(carry):
        n, done = carry
        return jnp.logical_and(n < n_blocks - 2, jnp.logical_not(done))

    def trip(carry):
        n, _ = carry
        scores(n + 2, 1); accumulate(n, 1); probs(0)
        scores(n + 3, 0); accumulate(n + 1, 0); probs(1)
        return n + 2, rest_is_negligible(n + 2)

    n_end, _ = lax.while_loop(more, trip, (jnp.int32(1), jnp.bool_(False)))
    accumulate(n_end, 1)
    acc = acc_scr[...]
    o_ref[...] = (acc[:, :LANES] / acc[:, LANES:]).astype(o_ref.dtype)


def _fox_attention(q, cq, k, ck, v, stats, batch, seq_len):
    n_q = seq_len // TQ
    q_spec = pl.BlockSpec((TQ, LANES), lambda b, p, i: (b * n_q + i, p))
    cq_spec = pl.BlockSpec((TQ, LANES), lambda b, p, i: (b * n_q + i, 0))
    kv_spec = pl.BlockSpec((seq_len, LANES), lambda b, p, i: (b, p))
    ck_spec = pl.BlockSpec((seq_len, LANES), lambda b, p, i: (b, 0))
    return pl.pallas_call(
        _fox_kernel,
        grid=(batch, N_PAIRS, n_q),
        in_specs=[pl.BlockSpec(memory_space=pltpu.SMEM), q_spec, cq_spec, kv_spec, ck_spec, kv_spec],
        out_specs=q_spec,
        out_shape=jax.ShapeDtypeStruct(q.shape, BF16),
        scratch_shapes=[
            pltpu.VMEM((HEADS_PER_STEP * TQ, 2 * LANES), BF16),
            pltpu.VMEM((HEADS_PER_STEP * TK, LANES), BF16),
            _slot_pair((HEADS_PER_STEP * TQ, TK), F32),
            _slot_pair((TQ, HEADS_PER_STEP * TK), BF16),
            _slot_pair((TQ, LANES), F32),
            pltpu.VMEM((HEADS_PER_STEP * TQ, LANES), F32),
            pltpu.VMEM((TQ, 2 * LANES), F32),
            pltpu.VMEM((HEADS_PER_STEP * TQ, LANES), F32),
            pltpu.VMEM((HEADS_PER_STEP * TQ, LANES), F32),
        ],
        compiler_params=pltpu.CompilerParams(
            dimension_semantics=("arbitrary", "arbitrary", "arbitrary"),
            vmem_limit_bytes=VMEM_LIMIT),
    )(stats, q, cq, k, ck, v)


def _post_kernel(h_ref, mix_ref, qm_ref, mkv_ref, wo_ref, g2_ref, w1_ref, w2_ref, gf_ref, o_ref,
                 *, final_norm):
    qm = qm_ref[...]
    mem_k = mkv_ref[0, :, :MEM_WIDTH]
    mem_v = mkv_ref[0, :, MEM_WIDTH:]
    lane = lax.broadcasted_iota(jnp.int32, (1, MEM_WIDTH), 1)
    zero = jnp.zeros_like(qm)
    mem_out = jnp.zeros(qm.shape, F32)
    for hd in range(N_MEM_HEADS):
        head = (lane >= hd * HEAD_DIM) & (lane < (hd + 1) * HEAD_DIM)
        s = _dot_nt(jnp.where(head, qm, zero), mem_k)
        p = jnp.exp(s - jnp.max(s, axis=1, keepdims=True))
        p = p / jnp.sum(p, axis=1, keepdims=True)
        mem_out = jnp.where(head, _dot(p.astype(BF16), mem_v), mem_out)

    merged = jnp.concatenate([mix_ref[...], mem_out.astype(BF16)], axis=1)
    h1 = h_ref[...] + _dot(merged, wo_ref[...])

    xn = _rms_norm_rows(h1, g2_ref[...]).astype(BF16)
    mlp = jnp.zeros(h1.shape, F32)
    for c0 in range(0, D_FF, FF_TILE):
        a = jnp.maximum(_dot(xn, w1_ref[:, c0:c0 + FF_TILE]), 0.0)
        mlp = mlp + _dot((a * a).astype(BF16), w2_ref[c0:c0 + FF_TILE, :])
    h2 = h1 + mlp
    if final_norm:
        h2 = _rms_norm_rows(h2, gf_ref[...])
    o_ref[...] = h2


def _post(h_rows, mix, qm, mkv_l, wo, g2, w1, w2, gf, batch, final_norm):
    rows = h_rows.shape[0]
    tiles_per_seq = rows // batch // ROW_TILE
    n_mem = mkv_l.shape[1]
    row_spec = lambda wd: pl.BlockSpec((ROW_TILE, wd), lambda i: (i, 0))
    single = pl.Buffered(1)
    const = lambda shape: pl.BlockSpec(shape, lambda i: (0,) * len(shape), pipeline_mode=single)
    return pl.pallas_call(
        functools.partial(_post_kernel, final_norm=final_norm),
        grid=(rows // ROW_TILE,),
        in_specs=[
            row_spec(D_MODEL),
            row_spec(MIX_WIDTH),
            row_spec(MEM_WIDTH),
            pl.BlockSpec((1, n_mem, 2 * MEM_WIDTH), lambda i: (i // tiles_per_seq, 0, 0)),
            const(wo.shape),
            const((1, D_MODEL)),
            const(w1.shape),
            const(w2.shape),
            const((1, D_MODEL)),
        ],
        out_specs=row_spec(D_MODEL),
        out_shape=jax.ShapeDtypeStruct(h_rows.shape, F32),
        compiler_params=pltpu.CompilerParams(
            dimension_semantics=("arbitrary",), vmem_limit_bytes=VMEM_LIMIT),
    )(h_rows, mix, qm, mkv_l, wo, g2.reshape(1, D_MODEL), w1, w2, gf.reshape(1, D_MODEL))


def kernel(x, mem, norm1_g, w_in_a, w_in_b, w_mem_kv, mem_norm_g, w_o, norm2_g, w_mlp1, w_mlp2,
           kv_norm_g, w_kv_shared, b_f, final_norm_g):
    batch, seq_len, _ = x.shape
    n_mem = mem.shape[1]
    depth = norm1_g.shape[0]
    n_a = w_in_a.shape[0]
    assert seq_len % ROW_TILE == 0 and seq_len % TQ == 0 and TQ % TK == 0
    scale = HEAD_DIM ** -0.5

    q_scale_a = jnp.concatenate([jnp.full((MIX_WIDTH,), scale, F32), jnp.ones((2 * MIX_WIDTH,), F32),
                                 jnp.full((MEM_WIDTH,), scale, F32)])
    w_a = (w_in_a * q_scale_a).astype(BF16)
    w_b = (w_in_b * scale).astype(BF16)
    w_mkv = w_mem_kv.astype(BF16)
    w_out = w_o.astype(BF16)
    w1 = w_mlp1.astype(BF16)
    w2 = w_mlp2.astype(BF16)
    pad = LANES - N_MIX_HEADS
    w_kv = jnp.pad(w_kv_shared, ((0, 0), (0, pad))).astype(BF16)
    bf_pad = jnp.pad(b_f.astype(F32), (0, pad)).reshape(1, LANES)

    h = x.reshape(batch * seq_len, D_MODEL)
    mkv = _mem_kv(mem.reshape(batch * n_mem, D_MODEL), mem_norm_g, w_mkv)
    mkv = mkv.reshape(depth * batch, n_mem, 2 * MEM_WIDTH)

    k_sh = v_sh = cq = ck = stats = None
    for l in range(depth):
        if l == n_a:
            k_sh, v_sh, cq, ck, stats = _shared_kv(h, kv_norm_g, w_kv, bf_pad, seq_len)
            stats = stats.reshape(batch, seq_len // ROW_TILE, STATS_ROWS, LANES)[:, :, :2, :N_MIX_HEADS]
            stats = stats.reshape(batch, seq_len // ROW_TILE, 2 * N_MIX_HEADS)
        if l < n_a:
            q, k, v, qm = _norm_proj(h, norm1_g[l], w_a[l],
                                     (MIX_WIDTH, MIX_WIDTH, MIX_WIDTH, MEM_WIDTH),
                                     scales=(LOG2E, None, None, None))
            mix = _stick_attention(q, k, v, batch, seq_len)
        else:
            q, qm = _norm_proj(h, norm1_g[l], w_b[l - n_a], (MIX_WIDTH, MEM_WIDTH))
            mix = _fox_attention(q, cq, k_sh, ck, v_sh, stats, batch, seq_len)
        h = _post(h, mix, qm, mkv[l * batch:(l + 1) * batch], w_out[l], norm2_g[l], w1[l], w2[l],
                  final_norm_g, batch, final_norm=(l == depth - 1))
    return h.reshape(batch, seq_len, D_MODEL)
```

```python
import functools

import jax
import jax.numpy as jnp
from jax import lax
from jax.experimental import pallas as pl
from jax.experimental.pallas import tpu as pltpu

D_MODEL = 1024
HEAD_DIM = 64
N_MIX_HEADS = 8
N_MEM_HEADS = 4
MIX_WIDTH = N_MIX_HEADS * HEAD_DIM
MEM_WIDTH = N_MEM_HEADS * HEAD_DIM
D_FF = 4 * D_MODEL
EPS = 1e-6
NEG_INF = -1e30
LOG2E = 1.4426950408889634
STICK_SPENT_LOG2 = 160.0
FOX_NEGLIGIBLE = 110.0
NORM_MARGIN = 1.01
STATS_ROWS = 8

LANES = 128
HEADS_PER_STEP = LANES // HEAD_DIM
N_PAIRS = N_MIX_HEADS // HEADS_PER_STEP
GATE_LANES = 8
N_PIECES = 3

ROW_TILE = 512
TQ = 512
TK = 256
FF_TILE = 1024
VMEM_LIMIT = 56 * 1024 * 1024

BF16 = jnp.bfloat16
F32 = jnp.float32


def _dot(a, b):
    return jnp.dot(a, b, preferred_element_type=F32)


def _dot_nt(a, b):
    return lax.dot_general(a, b, (((1,), (1,)), ((), ())), preferred_element_type=F32)


def _rms_norm_rows(x, g):
    return x * lax.rsqrt(jnp.mean(x * x, axis=-1, keepdims=True) + EPS) * g


def _const_spec(shape):
    return pl.BlockSpec(shape, lambda *_: (0,) * len(shape))


def _slot_pair(shape, dtype):
    return (pltpu.VMEM(shape, dtype), pltpu.VMEM(shape, dtype))


def _norm_proj_kernel(x_ref, g_ref, w_ref, *o_refs, scales):
    xn = _rms_norm_rows(x_ref[...], g_ref[...]).astype(BF16)
    y = _dot(xn, w_ref[...])
    off = 0
    for o_ref, scale in zip(o_refs, scales):
        width = o_ref.shape[-1]
        part = y[:, off:off + width]
        o_ref[...] = (part if scale is None else part * scale).astype(o_ref.dtype)
        off += width


def _norm_proj(x, g, w, widths, scales=None):
    rows = x.shape[0]
    n = w.shape[1]
    assert sum(widths) == n and rows % ROW_TILE == 0
    scales = (None,) * len(widths) if scales is None else scales
    return pl.pallas_call(
        functools.partial(_norm_proj_kernel, scales=scales),
        grid=(rows // ROW_TILE,),
        in_specs=[
            pl.BlockSpec((ROW_TILE, D_MODEL), lambda i: (i, 0)),
            _const_spec((1, D_MODEL)),
            _const_spec((D_MODEL, n)),
        ],
        out_specs=[pl.BlockSpec((ROW_TILE, wd), lambda i: (i, 0)) for wd in widths],
        out_shape=[jax.ShapeDtypeStruct((rows, wd), BF16) for wd in widths],
        compiler_params=pltpu.CompilerParams(
            dimension_semantics=("arbitrary",), vmem_limit_bytes=VMEM_LIMIT),
    )(x, g.reshape(1, D_MODEL), w)


def _mem_kv_kernel(x_ref, g_ref, w_ref, o_ref):
    xn = _rms_norm_rows(x_ref[...], g_ref[0]).astype(BF16)
    o_ref[0] = _dot(xn, w_ref[0]).astype(o_ref.dtype)


def _mem_kv(mem_rows, gains, w):
    depth = w.shape[0]
    rows = mem_rows.shape[0]
    return pl.pallas_call(
        _mem_kv_kernel,
        grid=(depth,),
        in_specs=[
            _const_spec((rows, D_MODEL)),
            pl.BlockSpec((1, 1, D_MODEL), lambda l: (l, 0, 0)),
            pl.BlockSpec((1, D_MODEL, 2 * MEM_WIDTH), lambda l: (l, 0, 0)),
        ],
        out_specs=pl.BlockSpec((1, rows, 2 * MEM_WIDTH), lambda l: (l, 0, 0)),
        out_shape=jax.ShapeDtypeStruct((depth, rows, 2 * MEM_WIDTH), BF16),
        compiler_params=pltpu.CompilerParams(
            dimension_semantics=("arbitrary",), vmem_limit_bytes=VMEM_LIMIT),
    )(mem_rows, gains.reshape(depth, 1, D_MODEL), w)


def _split3(c):
    hi = c.astype(BF16)
    r1 = c - hi.astype(F32)
    mid = r1.astype(BF16)
    lo = (r1 - mid.astype(F32)).astype(BF16)
    return hi, mid, lo


def _shared_kv_kernel(x_ref, g_ref, w_ref, bf_ref, eq_ref, ek_ref, oneq_ref, onek_ref, ehead_ref,
                      k_ref, v_ref, cq_ref, ck_ref, stats_ref, carry_ref, knorm_ref, *, tiles_per_seq):
    i = pl.program_id(0)

    @pl.when(i % tiles_per_seq == 0)
    def _():
        carry_ref[...] = jnp.zeros_like(carry_ref)
        knorm_ref[...] = jnp.zeros_like(knorm_ref)

    xn = _rms_norm_rows(x_ref[...], g_ref[...]).astype(BF16)
    y = _dot(xn, w_ref[...])
    k_bf = y[:, :MIX_WIDTH].astype(BF16)
    k_ref[...] = k_bf
    v_ref[...] = y[:, MIX_WIDTH:2 * MIX_WIDTH].astype(BF16)

    k_f = k_bf.astype(F32)
    k_sq = _dot((k_f * k_f).astype(BF16), ehead_ref[...])
    knorm_ref[...] = jnp.maximum(knorm_ref[...], jnp.max(k_sq, axis=0, keepdims=True))

    f = y[:, 2 * MIX_WIDTH:] + bf_ref[...]
    log_f = jnp.minimum(f, 0.0) - jnp.log1p(jnp.exp(-jnp.abs(f)))
    lane = lax.broadcasted_iota(jnp.int32, log_f.shape, 1)
    log_f = jnp.where(lane < N_MIX_HEADS, log_f, 0.0)

    rows = log_f.shape[0]
    r = lax.broadcasted_iota(jnp.int32, (rows, rows), 0)
    c = lax.broadcasted_iota(jnp.int32, (rows, rows), 1)
    tri = (c <= r).astype(BF16)
    hi, mid, lo = _split3(log_f)
    cum = (_dot(tri, hi) + _dot(tri, mid)) + _dot(tri, lo) + carry_ref[...]
    carry_ref[...] = cum[rows - 1:rows, :]

    stats_ref[...] = jnp.concatenate(
        [cum[rows - 1:rows, :], jnp.sqrt(knorm_ref[...]) * NORM_MARGIN,
         jnp.zeros((STATS_ROWS - 2, LANES), F32)], axis=0)

    pieces = jnp.concatenate(_split3(cum), axis=1)
    cq_ref[...] = (_dot(pieces, eq_ref[...]) + oneq_ref[...]).astype(BF16)
    ck_ref[...] = (onek_ref[...] - _dot(pieces, ek_ref[...])).astype(BF16)


def _gate_scatter_constants():
    eq = jnp.zeros((N_PIECES * LANES, LANES), F32)
    ek = jnp.zeros((N_PIECES * LANES, LANES), F32)
    oneq = jnp.zeros((1, LANES), F32)
    onek = jnp.zeros((1, LANES), F32)
    for h in range(N_MIX_HEADS):
        for p in range(N_PIECES):
            eq = eq.at[p * LANES + h, GATE_LANES * h + p].set(1.0)
            ek = ek.at[p * LANES + h, GATE_LANES * h + N_PIECES + p].set(1.0)
            oneq = oneq.at[0, GATE_LANES * h + N_PIECES + p].set(1.0)
            onek = onek.at[0, GATE_LANES * h + p].set(1.0)
    return eq.astype(BF16), ek.astype(BF16), oneq, onek


def _head_indicator():
    lane = jnp.arange(MIX_WIDTH)[:, None] // HEAD_DIM
    return (lane == jnp.arange(LANES)[None, :]).astype(BF16)


def _shared_kv(h_rows, g, w_pad, bf_pad, seq_len):
    rows = h_rows.shape[0]
    n = w_pad.shape[1]
    eq, ek, oneq, onek = _gate_scatter_constants()
    ehead = _head_indicator()
    kern = functools.partial(_shared_kv_kernel, tiles_per_seq=seq_len // ROW_TILE)
    row_spec = lambda wd: pl.BlockSpec((ROW_TILE, wd), lambda i: (i, 0))
    return pl.pallas_call(
        kern,
        grid=(rows // ROW_TILE,),
        in_specs=[
            row_spec(D_MODEL),
            _const_spec((1, D_MODEL)),
            _const_spec((D_MODEL, n)),
            _const_spec((1, LANES)),
            _const_spec(eq.shape),
            _const_spec(ek.shape),
            _const_spec((1, LANES)),
            _const_spec((1, LANES)),
            _const_spec(ehead.shape),
        ],
        out_specs=[row_spec(MIX_WIDTH), row_spec(MIX_WIDTH), row_spec(LANES), row_spec(LANES),
                   pl.BlockSpec((STATS_ROWS, LANES), lambda i: (i, 0))],
        out_shape=[
            jax.ShapeDtypeStruct((rows, MIX_WIDTH), BF16),
            jax.ShapeDtypeStruct((rows, MIX_WIDTH), BF16),
            jax.ShapeDtypeStruct((rows, LANES), BF16),
            jax.ShapeDtypeStruct((rows, LANES), BF16),
            jax.ShapeDtypeStruct((rows // ROW_TILE * STATS_ROWS, LANES), F32),
        ],
        scratch_shapes=[pltpu.VMEM((1, LANES), F32), pltpu.VMEM((1, LANES), F32)],
        compiler_params=pltpu.CompilerParams(
            dimension_semantics=("arbitrary",), vmem_limit_bytes=VMEM_LIMIT),
    )(h_rows, g.reshape(1, D_MODEL), w_pad, bf_pad, eq, ek, oneq, onek, ehead)


def _head_lane_masks():
    lane = lax.broadcasted_iota(jnp.int32, (1, LANES), 1)
    return [(lane >= h * HEAD_DIM) & (lane < (h + 1) * HEAD_DIM) for h in range(HEADS_PER_STEP)]


def _stack_heads(x2, masks):
    zero = jnp.zeros_like(x2)
    return jnp.concatenate([jnp.where(m, x2, zero) for m in masks], axis=0)


def _heads_to_lanes(x):
    n = x.shape[0] // HEADS_PER_STEP
    return jnp.concatenate([x[h * n:(h + 1) * n] for h in range(HEADS_PER_STEP)], axis=1)


def _visible(t0, start, width, strict):
    t_pos = t0 + (lax.broadcasted_iota(jnp.int32, (HEADS_PER_STEP * TQ, width), 0) & (TQ - 1))
    s_pos = start + lax.broadcasted_iota(jnp.int32, (HEADS_PER_STEP * TQ, width), 1)
    return s_pos < t_pos if strict else s_pos <= t_pos


def _stick_kernel(q_ref, k_ref, v_ref, o_ref, q_scr, tri_scr, z_scr, zl_scr, w_scr, later_scr, acc_scr):
    t0 = pl.program_id(2) * TQ
    masks = _head_lane_masks()
    q_scr[...] = _stack_heads(q_ref[...], masks)
    r = lax.broadcasted_iota(jnp.int32, (TK, TK), 0)
    c = lax.broadcasted_iota(jnp.int32, (TK, TK), 1)
    tri_scr[...] = (r >= c).astype(BF16)
    later_scr[...] = jnp.zeros_like(later_scr)
    acc_scr[...] = jnp.zeros_like(acc_scr)

    def key_start(n):
        return pl.multiple_of(jnp.maximum(t0 + (TQ // TK - 1 - n) * TK, 0), TK)

    def visible(n):
        return _visible(t0, t0 + (TQ // TK - 1 - n) * TK, TK, strict=True)

    def scores(n, slot):
        z_scr[slot][...] = _dot_nt(q_scr[...], k_ref[pl.ds(key_start(n), TK), :])

    def weights(slot, mask=None):
        z = z_scr[slot][...]
        sp = jnp.maximum(z, 0.0) + jnp.log(1.0 + jnp.exp2(-jnp.abs(z))) * LOG2E
        if mask is not None:
            sp = jnp.where(mask, sp, 0.0)
        zl_scr[...] = z - later_scr[...]
        within = _dot(sp.astype(BF16), tri_scr[...])
        w = jnp.exp2(zl_scr[...] - within)
        if mask is not None:
            w = jnp.where(mask, w, 0.0)
        w_scr[slot][...] = _heads_to_lanes(w.astype(BF16))
        later_scr[...] += within[:, 0:1]

    def values(n, slot):
        acc_scr[...] += _dot(w_scr[slot][...], _stack_heads(v_ref[pl.ds(key_start(n), TK), :], masks))

    assert TQ == 2 * TK
    n_blocks = TQ // TK + t0 // TK
    scores(0, 0)
    weights(0, visible(0)); scores(1, 1)
    values(0, 0); weights(1, visible(1)); scores(2, 0)

    def more(carry):
        n, spent = carry
        return jnp.logical_and(n < n_blocks - 2, jnp.logical_not(spent))

    def trip(carry):
        n, _ = carry
        scores(n + 2, 1); values(n, 1); weights(0)
        scores(n + 3, 0); values(n + 1, 0); weights(1)
        return n + 2, jnp.min(later_scr[...]) >= STICK_SPENT_LOG2

    n_end, _ = lax.while_loop(more, trip, (jnp.int32(1), jnp.bool_(False)))
    values(n_end, 1)
    o_ref[...] = acc_scr[...].astype(o_ref.dtype)


def _stick_attention(q, k, v, batch, seq_len):
    n_q = seq_len // TQ
    q_spec = pl.BlockSpec((TQ, LANES), lambda b, p, i: (b * n_q + i, p))
    kv_spec = pl.BlockSpec((seq_len, LANES), lambda b, p, i: (b, p))
    return pl.pallas_call(
        _stick_kernel,
        grid=(batch, N_PAIRS, n_q),
        in_specs=[q_spec, kv_spec, kv_spec],
        out_specs=q_spec,
        out_shape=jax.ShapeDtypeStruct(q.shape, BF16),
        scratch_shapes=[
            pltpu.VMEM((HEADS_PER_STEP * TQ, LANES), BF16),
            pltpu.VMEM((TK, TK), BF16),
            _slot_pair((HEADS_PER_STEP * TQ, TK), F32),
            pltpu.VMEM((HEADS_PER_STEP * TQ, TK), F32),
            _slot_pair((TQ, HEADS_PER_STEP * TK), BF16),
            pltpu.VMEM((HEADS_PER_STEP * TQ, 1), F32),
            pltpu.VMEM((TQ, LANES), F32),
        ],
        compiler_params=pltpu.CompilerParams(
            dimension_semantics=("arbitrary", "arbitrary", "arbitrary"),
            vmem_limit_bytes=VMEM_LIMIT),
    )(q, k, v)


def _fox_kernel(stats_ref, q_ref, cq_ref, k_ref, ck_ref, v_ref, o_ref,
                q_scr, ones_scr, z_scr, p_scr, alpha_scr, m_scr, acc_scr, qnorm_scr, crow_scr):
    batch_idx = pl.program_id(0)
    pair = pl.program_id(1)
    t0 = pl.program_id(2) * TQ
    masks = _head_lane_masks()
    lane = lax.broadcasted_iota(jnp.int32, (1, LANES), 1)
    cq = cq_ref[...]
    gate_rows = []
    for h in range(HEADS_PER_STEP):
        g0 = (pair * HEADS_PER_STEP + h) * GATE_LANES
        gate_rows.append(jnp.where((lane >= g0) & (lane < g0 + GATE_LANES), cq, jnp.zeros_like(cq)))
    q_rows = _stack_heads(q_ref[...], masks)
    gates = jnp.concatenate(gate_rows, axis=0)
    q_scr[...] = jnp.concatenate([q_rows, gates], axis=1)
    lane_rows = lax.broadcasted_iota(jnp.int32, (HEADS_PER_STEP * TK, LANES), 1) // HEAD_DIM
    head_rows = lax.broadcasted_iota(jnp.int32, (HEADS_PER_STEP * TK, LANES), 0) // TK
    ones_scr[...] = (lane_rows == head_rows).astype(F32).astype(BF16)
    m_scr[...] = jnp.full_like(m_scr, NEG_INF)
    acc_scr[...] = jnp.zeros_like(acc_scr)

    q_f = q_rows.astype(F32)
    q_norm = jnp.sqrt(jnp.sum(q_f * q_f, axis=1, keepdims=True)) * NORM_MARGIN
    qnorm_scr[...] = jnp.broadcast_to(q_norm, qnorm_scr.shape)
    piece_lanes = (lane % GATE_LANES) < N_PIECES
    c_row = jnp.sum(jnp.where(piece_lanes, gates.astype(F32), 0.0), axis=1, keepdims=True)
    crow_scr[...] = jnp.broadcast_to(c_row, crow_scr.shape)

    def key_start(n):
        back = jnp.maximum(t0 - (n - 1) * TK, 0)
        return pl.multiple_of(jnp.where(n < 2, t0 + n * TK, back), TK)

    def visible(n):
        return _visible(t0, t0 + n * TK, TK, strict=False)

    def scores(n, slot):
        start = key_start(n)
        k_blk = jnp.concatenate([k_ref[pl.ds(start, TK), :], ck_ref[pl.ds(start, TK), :]], axis=1)
        z_scr[slot][...] = _dot_nt(q_scr[...], k_blk)

    def probs(slot, mask=None):
        z = z_scr[slot][...]
        if mask is not None:
            z = jnp.where(mask, z, NEG_INF)
        m = m_scr[...]
        m_new = jnp.maximum(m, jnp.max(z, axis=1, keepdims=True))
        m_scr[...] = m_new
        shift = jnp.concatenate([m_new] * (TK // LANES), axis=1)
        p_scr[slot][...] = _heads_to_lanes(jnp.exp(z - shift).astype(BF16))
        alpha = jnp.exp(m - m_new)
        alpha_scr[slot][...] = jnp.where(masks[0], alpha[:TQ], alpha[TQ:])

    def accumulate(n, slot):
        v_rows = jnp.concatenate(
            [_stack_heads(v_ref[pl.ds(key_start(n), TK), :], masks), ones_scr[...]], axis=1)
        alpha = alpha_scr[slot][...]
        acc_scr[...] = (jnp.concatenate([alpha, alpha], axis=1) * acc_scr[...]
                        + _dot(p_scr[slot][...], v_rows))

    def rest_is_negligible(n_done):
        last_key = key_start(n_done) - 1
        tile = jnp.maximum(last_key, 0) // ROW_TILE
        worst = None
        for h in range(HEADS_PER_STEP):
            head = pair * HEADS_PER_STEP + h
            c_floor = stats_ref[batch_idx, tile, head]
            k_norm = stats_ref[batch_idx, tile, N_MIX_HEADS + head]
            rows = slice(h * TQ, (h + 1) * TQ)
            gap = (qnorm_scr[rows, :] * k_norm + (crow_scr[rows, :] - c_floor)
                   - m_scr[rows, :] + FOX_NEGLIGIBLE)
            top = jnp.max(gap)
            worst = top if worst is None else jnp.maximum(worst, top)
        return worst <= 0.0

    assert TQ == 2 * TK
    n_blocks = TQ // TK + t0 // TK
    scores(0, 0)
    probs(0, visible(0)); scores(1, 1)
    accumulate(0, 0); probs(1, visible(1)); scores(2, 0)

    def more(carry):
        n, done = carry
        return jnp.logical_and(n < n_blocks - 2, jnp.logical_not(done))

    def trip(carry):
        n, _ = carry
        scores(n + 2, 1); accumulate(n, 1); probs(0)
        scores(n + 3, 0); accumulate(n + 1, 0); probs(1)
        return n + 2, rest_is_negligible(n + 2)

    n_end, _ = lax.while_loop(more, trip, (jnp.int32(1), jnp.bool_(False)))
    accumulate(n_end, 1)
    acc = acc_scr[...]
    o_ref[...] = (acc[:, :LANES] / acc[:, LANES:]).astype(o_ref.dtype)


def _fox_attention(q, cq, k, ck, v, stats, batch, seq_len):
    n_q = seq_len // TQ
    q_spec = pl.BlockSpec((TQ, LANES), lambda b, p, i: (b * n_q + i, p))
    cq_spec = pl.BlockSpec((TQ, LANES), lambda b, p, i: (b * n_q + i, 0))
    kv_spec = pl.BlockSpec((seq_len, LANES), lambda b, p, i: (b, p))
    ck_spec = pl.BlockSpec((seq_len, LANES), lambda b, p, i: (b, 0))
    return pl.pallas_call(
        _fox_kernel,
        grid=(batch, N_PAIRS, n_q),
        in_specs=[pl.BlockSpec(memory_space=pltpu.SMEM), q_spec, cq_spec, kv_spec, ck_spec, kv_spec],
        out_specs=q_spec,
        out_shape=jax.ShapeDtypeStruct(q.shape, BF16),
        scratch_shapes=[
            pltpu.VMEM((HEADS_PER_STEP * TQ, 2 * LANES), BF16),
            pltpu.VMEM((HEADS_PER_STEP * TK, LANES), BF16),
            _slot_pair((HEADS_PER_STEP * TQ, TK), F32),
            _slot_pair((TQ, HEADS_PER_STEP * TK), BF16),
            _slot_pair((TQ, LANES), F32),
            pltpu.VMEM((HEADS_PER_STEP * TQ, LANES), F32),
            pltpu.VMEM((TQ, 2 * LANES), F32),
            pltpu.VMEM((HEADS_PER_STEP * TQ, LANES), F32),
            pltpu.VMEM((HEADS_PER_STEP * TQ, LANES), F32),
        ],
        compiler_params=pltpu.CompilerParams(
            dimension_semantics=("arbitrary", "arbitrary", "arbitrary"),
            vmem_limit_bytes=VMEM_LIMIT),
    )(stats, q, cq, k, ck, v)


def _post_kernel(h_ref, mix_ref, qm_ref, mkv_ref, wo_ref, g2_ref, w1_ref, w2_ref, gf_ref, o_ref,
                 *, final_norm):
    qm = qm_ref[...]
    mem_k = mkv_ref[0, :, :MEM_WIDTH]
    mem_v = mkv_ref[0, :, MEM_WIDTH:]
    lane = lax.broadcasted_iota(jnp.int32, (1, MEM_WIDTH), 1)
    zero = jnp.zeros_like(qm)
    mem_out = jnp.zeros(qm.shape, F32)
    for hd in range(N_MEM_HEADS):
        head = (lane >= hd * HEAD_DIM) & (lane < (hd + 1) * HEAD_DIM)
        s = _dot_nt(jnp.where(head, qm, zero), mem_k)
        p = jnp.exp(s - jnp.max(s, axis=1, keepdims=True))
        p = p / jnp.sum(p, axis=1, keepdims=True)
        mem_out = jnp.where(head, _dot(p.astype(BF16), mem_v), mem_out)

    merged = jnp.concatenate([mix_ref[...], mem_out.astype(BF16)], axis=1)
    h1 = h_ref[...] + _dot(merged, wo_ref[...])

    xn = _rms_norm_rows(h1, g2_ref[...]).astype(BF16)
    mlp = jnp.zeros(h1.shape, F32)
    for c0 in range(0, D_FF, FF_TILE):
        a = jnp.maximum(_dot(xn, w1_ref[:, c0:c0 + FF_TILE]), 0.0)
        mlp = mlp + _dot((a * a).astype(BF16), w2_ref[c0:c0 + FF_TILE, :])
    h2 = h1 + mlp
    if final_norm:
        h2 = _rms_norm_rows(h2, gf_ref[...])
    o_ref[...] = h2


def _post(h_rows, mix, qm, mkv_l, wo, g2, w1, w2, gf, batch, final_norm):
    rows = h_rows.shape[0]
    tiles_per_seq = rows // batch // ROW_TILE
    n_mem = mkv_l.shape[1]
    row_spec = lambda wd: pl.BlockSpec((ROW_TILE, wd), lambda i: (i, 0))
    single = pl.Buffered(1)
    const = lambda shape: pl.BlockSpec(shape, lambda i: (0,) * len(shape), pipeline_mode=single)
    return pl.pallas_call(
        functools.partial(_post_kernel, final_norm=final_norm),
        grid=(rows // ROW_TILE,),
        in_specs=[
            row_spec(D_MODEL),
            row_spec(MIX_WIDTH),
            row_spec(MEM_WIDTH),
            pl.BlockSpec((1, n_mem, 2 * MEM_WIDTH), lambda i: (i // tiles_per_seq, 0, 0)),
            const(wo.shape),
            const((1, D_MODEL)),
            const(w1.shape),
            const(w2.shape),
            const((1, D_MODEL)),
        ],
        out_specs=row_spec(D_MODEL),
        out_shape=jax.ShapeDtypeStruct(h_rows.shape, F32),
        compiler_params=pltpu.CompilerParams(
            dimension_semantics=("arbitrary",), vmem_limit_bytes=VMEM_LIMIT),
    )(h_rows, mix, qm, mkv_l, wo, g2.reshape(1, D_MODEL), w1, w2, gf.reshape(1, D_MODEL))


def kernel(x, mem, norm1_g, w_in_a, w_in_b, w_mem_kv, mem_norm_g, w_o, norm2_g, w_mlp1, w_mlp2,
           kv_norm_g, w_kv_shared, b_f, final_norm_g):
    batch, seq_len, _ = x.shape
    n_mem = mem.shape[1]
    depth = norm1_g.shape[0]
    n_a = w_in_a.shape[0]
    assert seq_len % ROW_TILE == 0 and seq_len % TQ == 0 and TQ % TK == 0
    scale = HEAD_DIM ** -0.5

    q_scale_a = jnp.concatenate([jnp.full((MIX_WIDTH,), scale, F32), jnp.ones((2 * MIX_WIDTH,), F32),
                                 jnp.full((MEM_WIDTH,), scale, F32)])
    w_a = (w_in_a * q_scale_a).astype(BF16)
    w_b = (w_in_b * scale).astype(BF16)
    w_mkv = w_mem_kv.astype(BF16)
    w_out = w_o.astype(BF16)
    w1 = w_mlp1.astype(BF16)
    w2 = w_mlp2.astype(BF16)
    pad = LANES - N_MIX_HEADS
    w_kv = jnp.pad(w_kv_shared, ((0, 0), (0, pad))).astype(BF16)
    bf_pad = jnp.pad(b_f.astype(F32), (0, pad)).reshape(1, LANES)

    h = x.reshape(batch * seq_len, D_MODEL)
    mkv = _mem_kv(mem.reshape(batch * n_mem, D_MODEL), mem_norm_g, w_mkv)
    mkv = mkv.reshape(depth * batch, n_mem, 2 * MEM_WIDTH)

    k_sh = v_sh = cq = ck = stats = None
    for l in range(depth):
        if l == n_a:
            k_sh, v_sh, cq, ck, stats = _shared_kv(h, kv_norm_g, w_kv, bf_pad, seq_len)
            stats = stats.reshape(batch, seq_len // ROW_TILE, STATS_ROWS, LANES)[:, :, :2, :N_MIX_HEADS]
            stats = stats.reshape(batch, seq_len // ROW_TILE, 2 * N_MIX_HEADS)
        if l < n_a:
            q, k, v, qm = _norm_proj(h, norm1_g[l], w_a[l],
                                     (MIX_WIDTH, MIX_WIDTH, MIX_WIDTH, MEM_WIDTH),
                                     scales=(LOG2E, None, None, None))
            mix = _stick_attention(q, k, v, batch, seq_len)
        else:
            q, qm = _norm_proj(h, norm1_g[l], w_b[l - n_a], (MIX_WIDTH, MEM_WIDTH))
            mix = _fox_attention(q, cq, k_sh, ck, v_sh, stats, batch, seq_len)
        h = _post(h, mix, qm, mkv[l * batch:(l + 1) * batch], w_out[l], norm2_g[l], w1[l], w2[l],
                  final_norm_g, batch, final_norm=(l == depth - 1))
    return h.reshape(batch, seq_len, D_MODEL)
```

```python
import functools

import jax
import jax.numpy as jnp
from jax import lax
from jax.experimental import pallas as pl
from jax.experimental.pallas import tpu as pltpu

D_MODEL = 1024
HEAD_DIM = 64
N_MIX_HEADS = 8
N_MEM_HEADS = 4
MIX_WIDTH = N_MIX_HEADS * HEAD_DIM
MEM_WIDTH = N_MEM_HEADS * HEAD_DIM
D_FF = 4 * D_MODEL
EPS = 1e-6
NEG_INF = -1e30
LOG2E = 1.4426950408889634
STICK_SPENT_LOG2 = 160.0
FOX_NEGLIGIBLE = 110.0
NORM_MARGIN = 1.01
STATS_ROWS = 8

LANES = 128
HEADS_PER_STEP = LANES // HEAD_DIM
N_PAIRS = N_MIX_HEADS // HEADS_PER_STEP
GATE_LANES = 8
N_PIECES = 3

ROW_TILE = 512
TQ = 512
TK = 256
FF_TILE = 1024
VMEM_LIMIT = 56 * 1024 * 1024

BF16 = jnp.bfloat16
F32 = jnp.float32


def _dot(a, b):
    return jnp.dot(a, b, preferred_element_type=F32)


def _dot_nt(a, b):
    return lax.dot_general(a, b, (((1,), (1,)), ((), ())), preferred_element_type=F32)


def _rms_norm_rows(x, g):
    return x * lax.rsqrt(jnp.mean(x * x, axis=-1, keepdims=True) + EPS) * g


def _const_spec(shape):
    return pl.BlockSpec(shape, lambda *_: (0,) * len(shape))


def _slot_pair(shape, dtype):
    return (pltpu.VMEM(shape, dtype), pltpu.VMEM(shape, dtype))


def _norm_proj_kernel(x_ref, g_ref, w_ref, *o_refs, scales):
    xn = _rms_norm_rows(x_ref[...], g_ref[...]).astype(BF16)
    y = _dot(xn, w_ref[...])
    off = 0
    for o_ref, scale in zip(o_refs, scales):
        width = o_ref.shape[-1]
        part = y[:, off:off + width]
        o_ref[...] = (part if scale is None else part * scale).astype(o_ref.dtype)
        off += width


def _norm_proj(x, g, w, widths, scales=None):
    rows = x.shape[0]
    n = w.shape[1]
    assert sum(widths) == n and rows % ROW_TILE == 0
    scales = (None,) * len(widths) if scales is None else scales
    return pl.pallas_call(
        functools.partial(_norm_proj_kernel, scales=scales),
        grid=(rows // ROW_TILE,),
        in_specs=[
            pl.BlockSpec((ROW_TILE, D_MODEL), lambda i: (i, 0)),
            _const_spec((1, D_MODEL)),
            _const_spec((D_MODEL, n)),
        ],
        out_specs=[pl.BlockSpec((ROW_TILE, wd), lambda i: (i, 0)) for wd in widths],
        out_shape=[jax.ShapeDtypeStruct((rows, wd), BF16) for wd in widths],
        compiler_params=pltpu.CompilerParams(
            dimension_semantics=("arbitrary",), vmem_limit_bytes=VMEM_LIMIT),
    )(x, g.reshape(1, D_MODEL), w)


def _mem_kv_kernel(x_ref, g_ref, w_ref, o_ref):
    xn = _rms_norm_rows(x_ref[...], g_ref[0]).astype(BF16)
    o_ref[0] = _dot(xn, w_ref[0]).astype(o_ref.dtype)


def _mem_kv(mem_rows, gains, w):
    depth = w.shape[0]
    rows = mem_rows.shape[0]
    return pl.pallas_call(
        _mem_kv_kernel,
        grid=(depth,),
        in_specs=[
            _const_spec((rows, D_MODEL)),
            pl.BlockSpec((1, 1, D_MODEL), lambda l: (l, 0, 0)),
            pl.BlockSpec((1, D_MODEL, 2 * MEM_WIDTH), lambda l: (l, 0, 0)),
        ],
        out_specs=pl.BlockSpec((1, rows, 2 * MEM_WIDTH), lambda l: (l, 0, 0)),
        out_shape=jax.ShapeDtypeStruct((depth, rows, 2 * MEM_WIDTH), BF16),
        compiler_params=pltpu.CompilerParams(
            dimension_semantics=("arbitrary",), vmem_limit_bytes=VMEM_LIMIT),
    )(mem_rows, gains.reshape(depth, 1, D_MODEL), w)


def _split3(c):
    hi = c.astype(BF16)
    r1 = c - hi.astype(F32)
    mid = r1.astype(BF16)
    lo = (r1 - mid.astype(F32)).astype(BF16)
    return hi, mid, lo


def _shared_kv_kernel(x_ref, g_ref, w_ref, bf_ref, eq_ref, ek_ref, oneq_ref, onek_ref, ehead_ref,
                      k_ref, v_ref, cq_ref, ck_ref, stats_ref, carry_ref, knorm_ref, *, tiles_per_seq):
    i = pl.program_id(0)

    @pl.when(i % tiles_per_seq == 0)
    def _():
        carry_ref[...] = jnp.zeros_like(carry_ref)
        knorm_ref[...] = jnp.zeros_like(knorm_ref)

    xn = _rms_norm_rows(x_ref[...], g_ref[...]).astype(BF16)
    y = _dot(xn, w_ref[...])
    k_bf = y[:, :MIX_WIDTH].astype(BF16)
    k_ref[...] = k_bf
    v_ref[...] = y[:, MIX_WIDTH:2 * MIX_WIDTH].astype(BF16)

    k_f = k_bf.astype(F32)
    k_sq = _dot((k_f * k_f).astype(BF16), ehead_ref[...])
    knorm_ref[...] = jnp.maximum(knorm_ref[...], jnp.max(k_sq, axis=0, keepdims=True))

    f = y[:, 2 * MIX_WIDTH:] + bf_ref[...]
    log_f = jnp.minimum(f, 0.0) - jnp.log1p(jnp.exp(-jnp.abs(f)))
    lane = lax.broadcasted_iota(jnp.int32, log_f.shape, 1)
    log_f = jnp.where(lane < N_MIX_HEADS, log_f, 0.0)

    rows = log_f.shape[0]
    r = lax.broadcasted_iota(jnp.int32, (rows, rows), 0)
    c = lax.broadcasted_iota(jnp.int32, (rows, rows), 1)
    tri = (c <= r).astype(BF16)
    hi, mid, lo = _split3(log_f)
    cum = (_dot(tri, hi) + _dot(tri, mid)) + _dot(tri, lo) + carry_ref[...]
    carry_ref[...] = cum[rows - 1:rows, :]

    stats_ref[...] = jnp.concatenate(
        [cum[rows - 1:rows, :], jnp.sqrt(knorm_ref[...]) * NORM_MARGIN,
         jnp.zeros((STATS_ROWS - 2, LANES), F32)], axis=0)

    pieces = jnp.concatenate(_split3(cum), axis=1)
    cq_ref[...] = (_dot(pieces, eq_ref[...]) + oneq_ref[...]).astype(BF16)
    ck_ref[...] = (onek_ref[...] - _dot(pieces, ek_ref[...])).astype(BF16)


def _gate_scatter_constants():
    eq = jnp.zeros((N_PIECES * LANES, LANES), F32)
    ek = jnp.zeros((N_PIECES * LANES, LANES), F32)
    oneq = jnp.zeros((1, LANES), F32)
    onek = jnp.zeros((1, LANES), F32)
    for h in range(N_MIX_HEADS):
        for p in range(N_PIECES):
            eq = eq.at[p * LANES + h, GATE_LANES * h + p].set(1.0)
            ek = ek.at[p * LANES + h, GATE_LANES * h + N_PIECES + p].set(1.0)
            oneq = oneq.at[0, GATE_LANES * h + N_PIECES + p].set(1.0)
            onek = onek.at[0, GATE_LANES * h + p].set(1.0)
    return eq.astype(BF16), ek.astype(BF16), oneq, onek


def _head_indicator():
    lane = jnp.arange(MIX_WIDTH)[:, None] // HEAD_DIM
    return (lane == jnp.arange(LANES)[None, :]).astype(BF16)


def _shared_kv(h_rows, g, w_pad, bf_pad, seq_len):
    rows = h_rows.shape[0]
    n = w_pad.shape[1]
    eq, ek, oneq, onek = _gate_scatter_constants()
    ehead = _head_indicator()
    kern = functools.partial(_shared_kv_kernel, tiles_per_seq=seq_len // ROW_TILE)
    row_spec = lambda wd: pl.BlockSpec((ROW_TILE, wd), lambda i: (i, 0))
    return pl.pallas_call(
        kern,
        grid=(rows // ROW_TILE,),
        in_specs=[
            row_spec(D_MODEL),
            _const_spec((1, D_MODEL)),
            _const_spec((D_MODEL, n)),
            _const_spec((1, LANES)),
            _const_spec(eq.shape),
            _const_spec(ek.shape),
            _const_spec((1, LANES)),
            _const_spec((1, LANES)),
            _const_spec(ehead.shape),
        ],
        out_specs=[row_spec(MIX_WIDTH), row_spec(MIX_WIDTH), row_spec(LANES), row_spec(LANES),
                   pl.BlockSpec((STATS_ROWS, LANES), lambda i: (i, 0))],
        out_shape=[
            jax.ShapeDtypeStruct((rows, MIX_WIDTH), BF16),
            jax.ShapeDtypeStruct((rows, MIX_WIDTH), BF16),
            jax.ShapeDtypeStruct((rows, LANES), BF16),
            jax.ShapeDtypeStruct((rows, LANES), BF16),
            jax.ShapeDtypeStruct((rows // ROW_TILE * STATS_ROWS, LANES), F32),
        ],
        scratch_shapes=[pltpu.VMEM((1, LANES), F32), pltpu.VMEM((1, LANES), F32)],
        compiler_params=pltpu.CompilerParams(
            dimension_semantics=("arbitrary",), vmem_limit_bytes=VMEM_LIMIT),
    )(h_rows, g.reshape(1, D_MODEL), w_pad, bf_pad, eq, ek, oneq, onek, ehead)


def _head_lane_masks():
    lane = lax.broadcasted_iota(jnp.int32, (1, LANES), 1)
    return [(lane >= h * HEAD_DIM) & (lane < (h + 1) * HEAD_DIM) for h in range(HEADS_PER_STEP)]


def _stack_heads(x2, masks):
    zero = jnp.zeros_like(x2)
    return jnp.concatenate([jnp.where(m, x2, zero) for m in masks], axis=0)


def _heads_to_lanes(x):
    n = x.shape[0] // HEADS_PER_STEP
    return jnp.concatenate([x[h * n:(h + 1) * n] for h in range(HEADS_PER_STEP)], axis=1)


def _visible(t0, start, width, strict):
    t_pos = t0 + (lax.broadcasted_iota(jnp.int32, (HEADS_PER_STEP * TQ, width), 0) & (TQ - 1))
    s_pos = start + lax.broadcasted_iota(jnp.int32, (HEADS_PER_STEP * TQ, width), 1)
    return s_pos < t_pos if strict else s_pos <= t_pos


def _stick_kernel(q_ref, k_ref, v_ref, o_ref, q_scr, tri_scr, z_scr, zl_scr, w_scr, later_scr, acc_scr):
    t0 = pl.program_id(2) * TQ
    masks = _head_lane_masks()
    q_scr[...] = _stack_heads(q_ref[...], masks)
    r = lax.broadcasted_iota(jnp.int32, (TK, TK), 0)
    c = lax.broadcasted_iota(jnp.int32, (TK, TK), 1)
    tri_scr[...] = (r >= c).astype(BF16)
    later_scr[...] = jnp.zeros_like(later_scr)
    acc_scr[...] = jnp.zeros_like(acc_scr)

    def key_start(n):
        return pl.multiple_of(jnp.maximum(t0 + (TQ // TK - 1 - n) * TK, 0), TK)

    def visible(n):
        return _visible(t0, t0 + (TQ // TK - 1 - n) * TK, TK, strict=True)

    def scores(n, slot):
        z_scr[slot][...] = _dot_nt(q_scr[...], k_ref[pl.ds(key_start(n), TK), :])

    def weights(slot, mask=None):
        z = z_scr[slot][...]
        sp = jnp.maximum(z, 0.0) + jnp.log(1.0 + jnp.exp2(-jnp.abs(z))) * LOG2E
        if mask is not None:
            sp = jnp.where(mask, sp, 0.0)
        zl_scr[...] = z - later_scr[...]
        within = _dot(sp.astype(BF16), tri_scr[...])
        w = jnp.exp2(zl_scr[...] - within)
        if mask is not None:
            w = jnp.where(mask, w, 0.0)
        w_scr[slot][...] = _heads_to_lanes(w.astype(BF16))
        later_scr[...] += within[:, 0:1]

    def values(n, slot):
        acc_scr[...] += _dot(w_scr[slot][...], _stack_heads(v_ref[pl.ds(key_start(n), TK), :], masks))

    assert TQ == 2 * TK
    n_blocks = TQ // TK + t0 // TK
    scores(0, 0)
    weights(0, visible(0)); scores(1, 1)
    values(0, 0); weights(1, visible(1)); scores(2, 0)

    def more(carry):
        n, spent = carry
        return jnp.logical_and(n < n_blocks - 2, jnp.logical_not(spent))

    def stick_spent():
        return jnp.min(later_scr[...]) >= STICK_SPENT_LOG2

    def trip(carry):
        n, _ = carry
        scores(n + 2, 1); values(n, 1); weights(0)
        spent_early = stick_spent()

        @pl.when(jnp.logical_not(spent_early))
        def _():
            scores(n + 3, 0); values(n + 1, 0); weights(1)

        return jnp.where(spent_early, n + 1, n + 2), stick_spent()

    n_end, _ = lax.while_loop(more, trip, (jnp.int32(1), jnp.bool_(False)))

    @pl.when(n_end % 2 == 1)
    def _():
        values(n_end, 1)

    @pl.when(n_end % 2 == 0)
    def _():
        values(n_end, 0)

    o_ref[...] = acc_scr[...].astype(o_ref.dtype)


def _stick_attention(q, k, v, batch, seq_len):
    n_q = seq_len // TQ
    q_spec = pl.BlockSpec((TQ, LANES), lambda b, p, i: (b * n_q + i, p))
    kv_spec = pl.BlockSpec((seq_len, LANES), lambda b, p, i: (b, p))
    return pl.pallas_call(
        _stick_kernel,
        grid=(batch, N_PAIRS, n_q),
        in_specs=[q_spec, kv_spec, kv_spec],
        out_specs=q_spec,
        out_shape=jax.ShapeDtypeStruct(q.shape, BF16),
        scratch_shapes=[
            pltpu.VMEM((HEADS_PER_STEP * TQ, LANES), BF16),
            pltpu.VMEM((TK, TK), BF16),
            _slot_pair((HEADS_PER_STEP * TQ, TK), F32),
            pltpu.VMEM((HEADS_PER_STEP * TQ, TK), F32),
            _slot_pair((TQ, HEADS_PER_STEP * TK), BF16),
            pltpu.VMEM((HEADS_PER_STEP * TQ, 1), F32),
            pltpu.VMEM((TQ, LANES), F32),
        ],
        compiler_params=pltpu.CompilerParams(
            dimension_semantics=("arbitrary", "arbitrary", "arbitrary"),
            vmem_limit_bytes=VMEM_LIMIT),
    )(q, k, v)


def _fox_kernel(stats_ref, q_ref, cq_ref, k_ref, ck_ref, v_ref, o_ref,
                q_scr, ones_scr, z_scr, p_scr, alpha_scr, m_scr, acc_scr, qnorm_scr, crow_scr):
    batch_idx = pl.program_id(0)
    pair = pl.program_id(1)
    t0 = pl.program_id(2) * TQ
    masks = _head_lane_masks()
    lane = lax.broadcasted_iota(jnp.int32, (1, LANES), 1)
    cq = cq_ref[...]
    gate_rows = []
    for h in range(HEADS_PER_STEP):
        g0 = (pair * HEADS_PER_STEP + h) * GATE_LANES
        gate_rows.append(jnp.where((lane >= g0) & (lane < g0 + GATE_LANES), cq, jnp.zeros_like(cq)))
    q_rows = _stack_heads(q_ref[...], masks)
    gates = jnp.concatenate(gate_rows, axis=0)
    q_scr[...] = jnp.concatenate([q_rows, gates], axis=1)
    lane_rows = lax.broadcasted_iota(jnp.int32, (HEADS_PER_STEP * TK, LANES), 1) // HEAD_DIM
    head_rows = lax.broadcasted_iota(jnp.int32, (HEADS_PER_STEP * TK, LANES), 0) // TK
    ones_scr[...] = (lane_rows == head_rows).astype(F32).astype(BF16)
    m_scr[...] = jnp.full_like(m_scr, NEG_INF)
    acc_scr[...] = jnp.zeros_like(acc_scr)

    q_f = q_rows.astype(F32)
    q_norm = jnp.sqrt(jnp.sum(q_f * q_f, axis=1, keepdims=True)) * NORM_MARGIN
    qnorm_scr[...] = jnp.broadcast_to(q_norm, qnorm_scr.shape)
    piece_lanes = (lane % GATE_LANES) < N_PIECES
    c_row = jnp.sum(jnp.where(piece_lanes, gates.astype(F32), 0.0), axis=1, keepdims=True)
    crow_scr[...] = jnp.broadcast_to(c_row, crow_scr.shape)

    def key_start(n):
        back = jnp.maximum(t0 - (n - 1) * TK, 0)
        return pl.multiple_of(jnp.where(n < 2, t0 + n * TK, back), TK)

    def visible(n):
        return _visible(t0, t0 + n * TK, TK, strict=False)

    def scores(n, slot):
        start = key_start(n)
        k_blk = jnp.concatenate([k_ref[pl.ds(start, TK), :], ck_ref[pl.ds(start, TK), :]], axis=1)
        z_scr[slot][...] = _dot_nt(q_scr[...], k_blk)

    def probs(slot, mask=None):
        z = z_scr[slot][...]
        if mask is not None:
            z = jnp.where(mask, z, NEG_INF)
        m = m_scr[...]
        m_new = jnp.maximum(m, jnp.max(z, axis=1, keepdims=True))
        m_scr[...] = m_new
        shift = jnp.concatenate([m_new] * (TK // LANES), axis=1)
        p_scr[slot][...] = _heads_to_lanes(jnp.exp(z - shift).astype(BF16))
        alpha = jnp.exp(m - m_new)
        alpha_scr[slot][...] = jnp.where(masks[0], alpha[:TQ], alpha[TQ:])

    def accumulate(n, slot):
        v_rows = jnp.concatenate(
            [_stack_heads(v_ref[pl.ds(key_start(n), TK), :], masks), ones_scr[...]], axis=1)
        alpha = alpha_scr[slot][...]
        acc_scr[...] = (jnp.concatenate([alpha, alpha], axis=1) * acc_scr[...]
                        + _dot(p_scr[slot][...], v_rows))

    def rest_is_negligible(n_done):
        last_key = key_start(n_done) - 1
        tile = jnp.maximum(last_key, 0) // ROW_TILE
        worst = None
        for h in range(HEADS_PER_STEP):
            head = pair * HEADS_PER_STEP + h
            c_floor = stats_ref[batch_idx, tile, head]
            k_norm = stats_ref[batch_idx, tile, N_MIX_HEADS + head]
            rows = slice(h * TQ, (h + 1) * TQ)
            gap = (qnorm_scr[rows, :] * k_norm + (crow_scr[rows, :] - c_floor)
                   - m_scr[rows, :] + FOX_NEGLIGIBLE)
            top = jnp.max(gap)
            worst = top if worst is None else jnp.maximum(worst, top)
        return worst <= 0.0

    assert TQ == 2 * TK
    n_blocks = TQ // TK + t0 // TK
    scores(0, 0)
    probs(0, visible(0)); scores(1, 1)
    accumulate(0, 0); probs(1, visible(1)); scores(2, 0)

    def more(carry):
        n, done = carry
        return jnp.logical_and(n < n_blocks - 2, jnp.logical_not(done))

    def trip(carry):
        n, _ = carry
        scores(n + 2, 1); accumulate(n, 1); probs(0)
        scores(n + 3, 0); accumulate(n + 1, 0); probs(1)
        return n + 2, rest_is_negligible(n + 2)

    n_end, _ = lax.while_loop(more, trip, (jnp.int32(1), jnp.bool_(False)))
    accumulate(n_end, 1)
    acc = acc_scr[...]
    o_ref[...] = (acc[:, :LANES] / acc[:, LANES:]).astype(o_ref.dtype)


def _fox_attention(q, cq, k, ck, v, stats, batch, seq_len):
    n_q = seq_len // TQ
    q_spec = pl.BlockSpec((TQ, LANES), lambda b, p, i: (b * n_q + i, p))
    cq_spec = pl.BlockSpec((TQ, LANES), lambda b, p, i: (b * n_q + i, 0))
    kv_spec = pl.BlockSpec((seq_len, LANES), lambda b, p, i: (b, p))
    ck_spec = pl.BlockSpec((seq_len, LANES), lambda b, p, i: (b, 0))
    return pl.pallas_call(
        _fox_kernel,
        grid=(batch, N_PAIRS, n_q),
        in_specs=[pl.BlockSpec(memory_space=pltpu.SMEM), q_spec, cq_spec, kv_spec, ck_spec, kv_spec],
        out_specs=q_spec,
        out_shape=jax.ShapeDtypeStruct(q.shape, BF16),
        scratch_shapes=[
            pltpu.VMEM((HEADS_PER_STEP * TQ, 2 * LANES), BF16),
            pltpu.VMEM((HEADS_PER_STEP * TK, LANES), BF16),
            _slot_pair((HEADS_PER_STEP * TQ, TK), F32),
            _slot_pair((TQ, HEADS_PER_STEP * TK), BF16),
            _slot_pair((TQ, LANES), F32),
            pltpu.VMEM((HEADS_PER_STEP * TQ, LANES), F32),
            pltpu.VMEM((TQ, 2 * LANES), F32),
            pltpu.VMEM((HEADS_PER_STEP * TQ, LANES), F32),
            pltpu.VMEM((HEADS_PER_STEP * TQ, LANES), F32),
        ],
        compiler_params=pltpu.CompilerParams(
            dimension_semantics=("arbitrary", "arbitrary", "arbitrary"),
            vmem_limit_bytes=VMEM_LIMIT),
    )(stats, q, cq, k, ck, v)


def _post_kernel(h_ref, mix_ref, qm_ref, mkv_ref, wo_ref, g2_ref, w1_ref, w2_ref, gf_ref, o_ref,
                 *, final_norm):
    qm = qm_ref[...]
    mem_k = mkv_ref[0, :, :MEM_WIDTH]
    mem_v = mkv_ref[0, :, MEM_WIDTH:]
    lane = lax.broadcasted_iota(jnp.int32, (1, MEM_WIDTH), 1)
    zero = jnp.zeros_like(qm)
    mem_out = jnp.zeros(qm.shape, F32)
    for hd in range(N_MEM_HEADS):
        head = (lane >= hd * HEAD_DIM) & (lane < (hd + 1) * HEAD_DIM)
        s = _dot_nt(jnp.where(head, qm, zero), mem_k)
        p = jnp.exp(s - jnp.max(s, axis=1, keepdims=True))
        p = p / jnp.sum(p, axis=1, keepdims=True)
        mem_out = jnp.where(head, _dot(p.astype(BF16), mem_v), mem_out)

    merged = jnp.concatenate([mix_ref[...], mem_out.astype(BF16)], axis=1)
    h1 = h_ref[...] + _dot(merged, wo_ref[...])

    xn = _rms_norm_rows(h1, g2_ref[...]).astype(BF16)
    mlp = jnp.zeros(h1.shape, F32)
    for c0 in range(0, D_FF, FF_TILE):
        a = jnp.maximum(_dot(xn, w1_ref[:, c0:c0 + FF_TILE]), 0.0)
        mlp = mlp + _dot((a * a).astype(BF16), w2_ref[c0:c0 + FF_TILE, :])
    h2 = h1 + mlp
    if final_norm:
        h2 = _rms_norm_rows(h2, gf_ref[...])
    o_ref[...] = h2


def _post(h_rows, mix, qm, mkv_l, wo, g2, w1, w2, gf, batch, final_norm):
    rows = h_rows.shape[0]
    tiles_per_seq = rows // batch // ROW_TILE
    n_mem = mkv_l.shape[1]
    row_spec = lambda wd: pl.BlockSpec((ROW_TILE, wd), lambda i: (i, 0))
    single = pl.Buffered(1)
    const = lambda shape: pl.BlockSpec(shape, lambda i: (0,) * len(shape), pipeline_mode=single)
    return pl.pallas_call(
        functools.partial(_post_kernel, final_norm=final_norm),
        grid=(rows // ROW_TILE,),
        in_specs=[
            row_spec(D_MODEL),
            row_spec(MIX_WIDTH),
            row_spec(MEM_WIDTH),
            pl.BlockSpec((1, n_mem, 2 * MEM_WIDTH), lambda i: (i // tiles_per_seq, 0, 0)),
            const(wo.shape),
            const((1, D_MODEL)),
            const(w1.shape),
            const(w2.shape),
            const((1, D_MODEL)),
        ],
        out_specs=row_spec(D_MODEL),
        out_shape=jax.ShapeDtypeStruct(h_rows.shape, F32),
        compiler_params=pltpu.CompilerParams(
            dimension_semantics=("arbitrary",), vmem_limit_bytes=VMEM_LIMIT),
    )(h_rows, mix, qm, mkv_l, wo, g2.reshape(1, D_MODEL), w1, w2, gf.reshape(1, D_MODEL))


def kernel(x, mem, norm1_g, w_in_a, w_in_b, w_mem_kv, mem_norm_g, w_o, norm2_g, w_mlp1, w_mlp2,
           kv_norm_g, w_kv_shared, b_f, final_norm_g):
    batch, seq_len, _ = x.shape
    n_mem = mem.shape[1]
    depth = norm1_g.shape[0]
    n_a = w_in_a.shape[0]
    assert seq_len % ROW_TILE == 0 and seq_len % TQ == 0 and TQ % TK == 0
    scale = HEAD_DIM ** -0.5

    q_scale_a = jnp.concatenate([jnp.full((MIX_WIDTH,), scale, F32), jnp.ones((2 * MIX_WIDTH,), F32),
                                 jnp.full((MEM_WIDTH,), scale, F32)])
    w_a = (w_in_a * q_scale_a).astype(BF16)
    w_b = (w_in_b * scale).astype(BF16)
    w_mkv = w_mem_kv.astype(BF16)
    w_out = w_o.astype(BF16)
    w1 = w_mlp1.astype(BF16)
    w2 = w_mlp2.astype(BF16)
    pad = LANES - N_MIX_HEADS
    w_kv = jnp.pad(w_kv_shared, ((0, 0), (0, pad))).astype(BF16)
    bf_pad = jnp.pad(b_f.astype(F32), (0, pad)).reshape(1, LANES)

    h = x.reshape(batch * seq_len, D_MODEL)
    mkv = _mem_kv(mem.reshape(batch * n_mem, D_MODEL), mem_norm_g, w_mkv)
    mkv = mkv.reshape(depth * batch, n_mem, 2 * MEM_WIDTH)

    k_sh = v_sh = cq = ck = stats = None
    for l in range(depth):
        if l == n_a:
            k_sh, v_sh, cq, ck, stats = _shared_kv(h, kv_norm_g, w_kv, bf_pad, seq_len)
            stats = stats.reshape(batch, seq_len // ROW_TILE, STATS_ROWS, LANES)[:, :, :2, :N_MIX_HEADS]
            stats = stats.reshape(batch, seq_len // ROW_TILE, 2 * N_MIX_HEADS)
        if l < n_a:
            q, k, v, qm = _norm_proj(h, norm1_g[l], w_a[l],
                                     (MIX_WIDTH, MIX_WIDTH, MIX_WIDTH, MEM_WIDTH),
                                     scales=(LOG2E, None, None, None))
            mix = _stick_attention(q, k, v, batch, seq_len)
        else:
            q, qm = _norm_proj(h, norm1_g[l], w_b[l - n_a], (MIX_WIDTH, MEM_WIDTH))
            mix = _fox_attention(q, cq, k_sh, ck, v_sh, stats, batch, seq_len)
        h = _post(h, mix, qm, mkv[l * batch:(l + 1) * batch], w_out[l], norm2_g[l], w1[l], w2[l],
                  final_norm_g, batch, final_norm=(l == depth - 1))
    return h.reshape(batch, seq_len, D_MODEL)
```

```python
import functools

import jax
import jax.numpy as jnp
from jax import lax
from jax.experimental import pallas as pl
from jax.experimental.pallas import tpu as pltpu

D_MODEL = 1024
HEAD_DIM = 64
N_MIX_HEADS = 8
N_MEM_HEADS = 4
MIX_WIDTH = N_MIX_HEADS * HEAD_DIM
MEM_WIDTH = N_MEM_HEADS * HEAD_DIM
D_FF = 4 * D_MODEL
EPS = 1e-6
NEG_INF = -1e30
LOG2E = 1.4426950408889634
STICK_SPENT_LOG2 = 160.0
FOX_NEGLIGIBLE = 110.0
NORM_MARGIN = 1.01
STATS_ROWS = 8
N_STATS = 2

LANES = 128
HEADS_PER_STEP = LANES // HEAD_DIM
N_PAIRS = N_MIX_HEADS // HEADS_PER_STEP
GATE_LANES = 8
N_PIECES = 3

ROW_TILE = 512
TQ = 512
TK = 256
FF_TILE = 1024
VMEM_LIMIT = 56 * 1024 * 1024

BF16 = jnp.bfloat16
F32 = jnp.float32


def _dot(a, b):
    return jnp.dot(a, b, preferred_element_type=F32)


def _dot_nt(a, b):
    return lax.dot_general(a, b, (((1,), (1,)), ((), ())), preferred_element_type=F32)


def _rms_norm_rows(x, g):
    return x * lax.rsqrt(jnp.mean(x * x, axis=-1, keepdims=True) + EPS) * g


def _const_spec(shape):
    return pl.BlockSpec(shape, lambda *_: (0,) * len(shape))


def _slot_pair(shape, dtype):
    return (pltpu.VMEM(shape, dtype), pltpu.VMEM(shape, dtype))


def _norm_proj_kernel(x_ref, g_ref, w_ref, *o_refs, scales):
    xn = _rms_norm_rows(x_ref[...], g_ref[...]).astype(BF16)
    y = _dot(xn, w_ref[...])
    off = 0
    for o_ref, scale in zip(o_refs, scales):
        width = o_ref.shape[-1]
        part = y[:, off:off + width]
        o_ref[...] = (part if scale is None else part * scale).astype(o_ref.dtype)
        off += width


def _norm_proj(x, g, w, widths, scales=None):
    rows = x.shape[0]
    n = w.shape[1]
    assert sum(widths) == n and rows % ROW_TILE == 0
    scales = (None,) * len(widths) if scales is None else scales
    return pl.pallas_call(
        functools.partial(_norm_proj_kernel, scales=scales),
        grid=(rows // ROW_TILE,),
        in_specs=[
            pl.BlockSpec((ROW_TILE, D_MODEL), lambda i: (i, 0)),
            _const_spec((1, D_MODEL)),
            _const_spec((D_MODEL, n)),
        ],
        out_specs=[pl.BlockSpec((ROW_TILE, wd), lambda i: (i, 0)) for wd in widths],
        out_shape=[jax.ShapeDtypeStruct((rows, wd), BF16) for wd in widths],
        compiler_params=pltpu.CompilerParams(
            dimension_semantics=("arbitrary",), vmem_limit_bytes=VMEM_LIMIT),
    )(x, g.reshape(1, D_MODEL), w)


def _mem_kv_kernel(x_ref, g_ref, w_ref, o_ref):
    xn = _rms_norm_rows(x_ref[...], g_ref[0]).astype(BF16)
    o_ref[0] = _dot(xn, w_ref[0]).astype(o_ref.dtype)


def _mem_kv(mem_rows, gains, w):
    depth = w.shape[0]
    rows = mem_rows.shape[0]
    return pl.pallas_call(
        _mem_kv_kernel,
        grid=(depth,),
        in_specs=[
            _const_spec((rows, D_MODEL)),
            pl.BlockSpec((1, 1, D_MODEL), lambda l: (l, 0, 0)),
            pl.BlockSpec((1, D_MODEL, 2 * MEM_WIDTH), lambda l: (l, 0, 0)),
        ],
        out_specs=pl.BlockSpec((1, rows, 2 * MEM_WIDTH), lambda l: (l, 0, 0)),
        out_shape=jax.ShapeDtypeStruct((depth, rows, 2 * MEM_WIDTH), BF16),
        compiler_params=pltpu.CompilerParams(
            dimension_semantics=("arbitrary",), vmem_limit_bytes=VMEM_LIMIT),
    )(mem_rows, gains.reshape(depth, 1, D_MODEL), w)


def _split3(c):
    hi = c.astype(BF16)
    r1 = c - hi.astype(F32)
    mid = r1.astype(BF16)
    lo = (r1 - mid.astype(F32)).astype(BF16)
    return hi, mid, lo


def _shared_kv_kernel(x_ref, g_ref, w_ref, bf_ref, eq_ref, ek_ref, oneq_ref, onek_ref, ehead_ref,
                      k_ref, v_ref, cq_ref, ck_ref, stats_ref, carry_ref, knorm_ref, *, tiles_per_seq):
    i = pl.program_id(0)

    @pl.when(i % tiles_per_seq == 0)
    def _():
        carry_ref[...] = jnp.zeros_like(carry_ref)
        knorm_ref[...] = jnp.zeros_like(knorm_ref)

    xn = _rms_norm_rows(x_ref[...], g_ref[...]).astype(BF16)
    y = _dot(xn, w_ref[...])
    k_bf = y[:, :MIX_WIDTH].astype(BF16)
    k_ref[...] = k_bf
    v_ref[...] = y[:, MIX_WIDTH:2 * MIX_WIDTH].astype(BF16)

    k_f = k_bf.astype(F32)
    k_sq = _dot((k_f * k_f).astype(BF16), ehead_ref[...])
    knorm_ref[...] = jnp.maximum(knorm_ref[...], jnp.max(k_sq, axis=0, keepdims=True))

    f = y[:, 2 * MIX_WIDTH:] + bf_ref[...]
    log_f = jnp.minimum(f, 0.0) - jnp.log1p(jnp.exp(-jnp.abs(f)))
    lane = lax.broadcasted_iota(jnp.int32, log_f.shape, 1)
    log_f = jnp.where(lane < N_MIX_HEADS, log_f, 0.0)

    rows = log_f.shape[0]
    r = lax.broadcasted_iota(jnp.int32, (rows, rows), 0)
    c = lax.broadcasted_iota(jnp.int32, (rows, rows), 1)
    tri = (c <= r).astype(BF16)
    hi, mid, lo = _split3(log_f)
    cum = (_dot(tri, hi) + _dot(tri, mid)) + _dot(tri, lo) + carry_ref[...]
    carry_ref[...] = cum[rows - 1:rows, :]

    stats_ref[...] = jnp.concatenate(
        [cum[rows - 1:rows, :], jnp.sqrt(knorm_ref[...]) * NORM_MARGIN,
         jnp.zeros((STATS_ROWS - N_STATS, LANES), F32)], axis=0)

    pieces = jnp.concatenate(_split3(cum), axis=1)
    cq_ref[...] = (_dot(pieces, eq_ref[...]) + oneq_ref[...]).astype(BF16)
    ck_ref[...] = (onek_ref[...] - _dot(pieces, ek_ref[...])).astype(BF16)


def _gate_scatter_constants():
    eq = jnp.zeros((N_PIECES * LANES, LANES), F32)
    ek = jnp.zeros((N_PIECES * LANES, LANES), F32)
    oneq = jnp.zeros((1, LANES), F32)
    onek = jnp.zeros((1, LANES), F32)
    for h in range(N_MIX_HEADS):
        for p in range(N_PIECES):
            eq = eq.at[p * LANES + h, GATE_LANES * h + p].set(1.0)
            ek = ek.at[p * LANES + h, GATE_LANES * h + N_PIECES + p].set(1.0)
            oneq = oneq.at[0, GATE_LANES * h + N_PIECES + p].set(1.0)
            onek = onek.at[0, GATE_LANES * h + p].set(1.0)
    return eq.astype(BF16), ek.astype(BF16), oneq, onek


def _head_indicator():
    lane = jnp.arange(MIX_WIDTH)[:, None] // HEAD_DIM
    return (lane == jnp.arange(LANES)[None, :]).astype(BF16)


def _shared_kv(h_rows, g, w_pad, bf_pad, seq_len):
    rows = h_rows.shape[0]
    n = w_pad.shape[1]
    eq, ek, oneq, onek = _gate_scatter_constants()
    ehead = _head_indicator()
    kern = functools.partial(_shared_kv_kernel, tiles_per_seq=seq_len // ROW_TILE)
    row_spec = lambda wd: pl.BlockSpec((ROW_TILE, wd), lambda i: (i, 0))
    return pl.pallas_call(
        kern,
        grid=(rows // ROW_TILE,),
        in_specs=[
            row_spec(D_MODEL),
            _const_spec((1, D_MODEL)),
            _const_spec((D_MODEL, n)),
            _const_spec((1, LANES)),
            _const_spec(eq.shape),
            _const_spec(ek.shape),
            _const_spec((1, LANES)),
            _const_spec((1, LANES)),
            _const_spec(ehead.shape),
        ],
        out_specs=[row_spec(MIX_WIDTH), row_spec(MIX_WIDTH), row_spec(LANES), row_spec(LANES),
                   pl.BlockSpec((STATS_ROWS, LANES), lambda i: (i, 0))],
        out_shape=[
            jax.ShapeDtypeStruct((rows, MIX_WIDTH), BF16),
            jax.ShapeDtypeStruct((rows, MIX_WIDTH), BF16),
            jax.ShapeDtypeStruct((rows, LANES), BF16),
            jax.ShapeDtypeStruct((rows, LANES), BF16),
            jax.ShapeDtypeStruct((rows // ROW_TILE * STATS_ROWS, LANES), F32),
        ],
        scratch_shapes=[pltpu.VMEM((1, LANES), F32), pltpu.VMEM((1, LANES), F32)],
        compiler_params=pltpu.CompilerParams(
            dimension_semantics=("arbitrary",), vmem_limit_bytes=VMEM_LIMIT),
    )(h_rows, g.reshape(1, D_MODEL), w_pad, bf_pad, eq, ek, oneq, onek, ehead)


def _head_lane_masks():
    lane = lax.broadcasted_iota(jnp.int32, (1, LANES), 1)
    return [(lane >= h * HEAD_DIM) & (lane < (h + 1) * HEAD_DIM) for h in range(HEADS_PER_STEP)]


def _stack_heads(x2, masks):
    zero = jnp.zeros_like(x2)
    return jnp.concatenate([jnp.where(m, x2, zero) for m in masks], axis=0)


def _heads_to_lanes(x):
    n = x.shape[0] // HEADS_PER_STEP
    return jnp.concatenate([x[h * n:(h + 1) * n] for h in range(HEADS_PER_STEP)], axis=1)


def _visible(t0, start, width, strict):
    t_pos = t0 + (lax.broadcasted_iota(jnp.int32, (HEADS_PER_STEP * TQ, width), 0) & (TQ - 1))
    s_pos = start + lax.broadcasted_iota(jnp.int32, (HEADS_PER_STEP * TQ, width), 1)
    return s_pos < t_pos if strict else s_pos <= t_pos


def _stick_kernel(q_ref, k_ref, v_ref, o_ref, q_scr, tri_scr, z_scr, zl_scr, w_scr, later_scr, acc_scr):
    t0 = pl.program_id(2) * TQ
    masks = _head_lane_masks()
    q_scr[...] = _stack_heads(q_ref[...], masks)
    r = lax.broadcasted_iota(jnp.int32, (TK, TK), 0)
    c = lax.broadcasted_iota(jnp.int32, (TK, TK), 1)
    tri_scr[...] = (r >= c).astype(BF16)
    later_scr[...] = jnp.zeros_like(later_scr)
    acc_scr[...] = jnp.zeros_like(acc_scr)

    def key_start(n):
        return pl.multiple_of(jnp.maximum(t0 + (TQ // TK - 1 - n) * TK, 0), TK)

    def visible(n):
        return _visible(t0, t0 + (TQ // TK - 1 - n) * TK, TK, strict=True)

    def scores(n, slot):
        z_scr[slot][...] = _dot_nt(q_scr[...], k_ref[pl.ds(key_start(n), TK), :])

    def weights(slot, mask=None):
        z = z_scr[slot][...]
        sp = jnp.maximum(z, 0.0) + jnp.log(1.0 + jnp.exp2(-jnp.abs(z))) * LOG2E
        if mask is not None:
            sp = jnp.where(mask, sp, 0.0)
        zl_scr[...] = z - later_scr[...]
        within = _dot(sp.astype(BF16), tri_scr[...])
        w = jnp.exp2(zl_scr[...] - within)
        if mask is not None:
            w = jnp.where(mask, w, 0.0)
        w_scr[slot][...] = _heads_to_lanes(w.astype(BF16))
        later_scr[...] += within[:, 0:1]

    def values(n, slot):
        acc_scr[...] += _dot(w_scr[slot][...], _stack_heads(v_ref[pl.ds(key_start(n), TK), :], masks))

    assert TQ == 2 * TK
    n_blocks = TQ // TK + t0 // TK
    scores(0, 0)
    weights(0, visible(0)); scores(1, 1)
    values(0, 0); weights(1, visible(1)); scores(2, 0)

    def more(carry):
        n, spent = carry
        return jnp.logical_and(n < n_blocks - 2, jnp.logical_not(spent))

    def stick_spent():
        return jnp.min(later_scr[...]) >= STICK_SPENT_LOG2

    def trip(carry):
        n, _ = carry
        scores(n + 2, 1); values(n, 1); weights(0)
        spent_early = stick_spent()

        @pl.when(jnp.logical_not(spent_early))
        def _():
            scores(n + 3, 0); values(n + 1, 0); weights(1)

        return jnp.where(spent_early, n + 1, n + 2), stick_spent()

    n_end, _ = lax.while_loop(more, trip, (jnp.int32(1), jnp.bool_(False)))

    @pl.when(n_end % 2 == 1)
    def _():
        values(n_end, 1)

    @pl.when(n_end % 2 == 0)
    def _():
        values(n_end, 0)

    o_ref[...] = acc_scr[...].astype(o_ref.dtype)


def _stick_attention(q, k, v, batch, seq_len):
    n_q = seq_len // TQ
    q_spec = pl.BlockSpec((TQ, LANES), lambda b, p, i: (b * n_q + i, p))
    kv_spec = pl.BlockSpec((seq_len, LANES), lambda b, p, i: (b, p))
    return pl.pallas_call(
        _stick_kernel,
        grid=(batch, N_PAIRS, n_q),
        in_specs=[q_spec, kv_spec, kv_spec],
        out_specs=q_spec,
        out_shape=jax.ShapeDtypeStruct(q.shape, BF16),
        scratch_shapes=[
            pltpu.VMEM((HEADS_PER_STEP * TQ, LANES), BF16),
            pltpu.VMEM((TK, TK), BF16),
            _slot_pair((HEADS_PER_STEP * TQ, TK), F32),
            pltpu.VMEM((HEADS_PER_STEP * TQ, TK), F32),
            _slot_pair((TQ, HEADS_PER_STEP * TK), BF16),
            pltpu.VMEM((HEADS_PER_STEP * TQ, 1), F32),
            pltpu.VMEM((TQ, LANES), F32),
        ],
        compiler_params=pltpu.CompilerParams(
            dimension_semantics=("arbitrary", "arbitrary", "arbitrary"),
            vmem_limit_bytes=VMEM_LIMIT),
    )(q, k, v)


def _fox_kernel(stats_ref, q_ref, cq_ref, k_ref, ck_ref, v_ref, o_ref,
                q_scr, ones_scr, z_scr, p_scr, alpha_scr, m_scr, acc_scr, qnorm_scr, crow_scr):
    batch_idx = pl.program_id(0)
    pair = pl.program_id(1)
    t0 = pl.program_id(2) * TQ
    masks = _head_lane_masks()
    lane = lax.broadcasted_iota(jnp.int32, (1, LANES), 1)
    cq = cq_ref[...]
    gate_rows = []
    for h in range(HEADS_PER_STEP):
        g0 = (pair * HEADS_PER_STEP + h) * GATE_LANES
        gate_rows.append(jnp.where((lane >= g0) & (lane < g0 + GATE_LANES), cq, jnp.zeros_like(cq)))
    q_rows = _stack_heads(q_ref[...], masks)
    gates = jnp.concatenate(gate_rows, axis=0)
    q_scr[...] = jnp.concatenate([q_rows, gates], axis=1)
    lane_rows = lax.broadcasted_iota(jnp.int32, (HEADS_PER_STEP * TK, LANES), 1) // HEAD_DIM
    head_rows = lax.broadcasted_iota(jnp.int32, (HEADS_PER_STEP * TK, LANES), 0) // TK
    ones_scr[...] = (lane_rows == head_rows).astype(F32).astype(BF16)
    m_scr[...] = jnp.full_like(m_scr, NEG_INF)
    acc_scr[...] = jnp.zeros_like(acc_scr)

    q_f = q_rows.astype(F32)
    q_norm = jnp.sqrt(jnp.sum(q_f * q_f, axis=1, keepdims=True)) * NORM_MARGIN
    qnorm_scr[...] = jnp.broadcast_to(q_norm, qnorm_scr.shape)
    piece_lanes = (lane % GATE_LANES) < N_PIECES
    c_row = jnp.sum(jnp.where(piece_lanes, gates.astype(F32), 0.0), axis=1, keepdims=True)
    crow_scr[...] = jnp.broadcast_to(c_row, crow_scr.shape)

    def key_start(n):
        back = jnp.maximum(t0 - (n - 1) * TK, 0)
        return pl.multiple_of(jnp.where(n < 2, t0 + n * TK, back), TK)

    def visible(n):
        return _visible(t0, t0 + n * TK, TK, strict=False)

    def scores(n, slot):
        start = key_start(n)
        k_blk = jnp.concatenate([k_ref[pl.ds(start, TK), :], ck_ref[pl.ds(start, TK), :]], axis=1)
        z_scr[slot][...] = _dot_nt(q_scr[...], k_blk)

    def probs(slot, mask=None):
        z = z_scr[slot][...]
        if mask is not None:
            z = jnp.where(mask, z, NEG_INF)
        m = m_scr[...]
        m_new = jnp.maximum(m, jnp.max(z, axis=1, keepdims=True))
        m_scr[...] = m_new
        shift = jnp.concatenate([m_new] * (TK // LANES), axis=1)
        p_scr[slot][...] = _heads_to_lanes(jnp.exp(z - shift).astype(BF16))
        alpha = jnp.exp(m - m_new)
        alpha_scr[slot][...] = jnp.where(masks[0], alpha[:TQ], alpha[TQ:])

    def accumulate(n, slot):
        v_rows = jnp.concatenate(
            [_stack_heads(v_ref[pl.ds(key_start(n), TK), :], masks), ones_scr[...]], axis=1)
        alpha = alpha_scr[slot][...]
        acc_scr[...] = (jnp.concatenate([alpha, alpha], axis=1) * acc_scr[...]
                        + _dot(p_scr[slot][...], v_rows))

    def rest_is_negligible(n_done):
        last_key = key_start(n_done) - 1
        tile = jnp.maximum(last_key, 0) // ROW_TILE
        worst = None
        for h in range(HEADS_PER_STEP):
            head = pair * HEADS_PER_STEP + h
            c_floor = stats_ref[batch_idx, tile, head]
            k_norm = stats_ref[batch_idx, tile, N_MIX_HEADS + head]
            rows = slice(h * TQ, (h + 1) * TQ)
            gap = (qnorm_scr[rows, :] * k_norm + (crow_scr[rows, :] - c_floor)
                   - m_scr[rows, :] + FOX_NEGLIGIBLE)
            top = jnp.max(gap)
            worst = top if worst is None else jnp.maximum(worst, top)
        return worst <= 0.0

    assert TQ == 2 * TK
    n_blocks = TQ // TK + t0 // TK
    scores(0, 0)
    probs(0, visible(0)); scores(1, 1)
    accumulate(0, 0); probs(1, visible(1)); scores(2, 0)

    def more(carry):
        n, done = carry
        return jnp.logical_and(n < n_blocks - 2, jnp.logical_not(done))

    def trip(carry):
        n, _ = carry
        scores(n + 2, 1); accumulate(n, 1); probs(0)
        scores(n + 3, 0); accumulate(n + 1, 0); probs(1)
        return n + 2, rest_is_negligible(n + 2)

    n_end, _ = lax.while_loop(more, trip, (jnp.int32(1), jnp.bool_(False)))
    accumulate(n_end, 1)
    acc = acc_scr[...]
    o_ref[...] = (acc[:, :LANES] / acc[:, LANES:]).astype(o_ref.dtype)


def _fox_attention(q, cq, k, ck, v, stats, batch, seq_len):
    n_q = seq_len // TQ
    q_spec = pl.BlockSpec((TQ, LANES), lambda b, p, i: (b * n_q + i, p))
    cq_spec = pl.BlockSpec((TQ, LANES), lambda b, p, i: (b * n_q + i, 0))
    kv_spec = pl.BlockSpec((seq_len, LANES), lambda b, p, i: (b, p))
    ck_spec = pl.BlockSpec((seq_len, LANES), lambda b, p, i: (b, 0))
    return pl.pallas_call(
        _fox_kernel,
        grid=(batch, N_PAIRS, n_q),
        in_specs=[pl.BlockSpec(memory_space=pltpu.SMEM), q_spec, cq_spec, kv_spec, ck_spec, kv_spec],
        out_specs=q_spec,
        out_shape=jax.ShapeDtypeStruct(q.shape, BF16),
        scratch_shapes=[
            pltpu.VMEM((HEADS_PER_STEP * TQ, 2 * LANES), BF16),
            pltpu.VMEM((HEADS_PER_STEP * TK, LANES), BF16),
            _slot_pair((HEADS_PER_STEP * TQ, TK), F32),
            _slot_pair((TQ, HEADS_PER_STEP * TK), BF16),
            _slot_pair((TQ, LANES), F32),
            pltpu.VMEM((HEADS_PER_STEP * TQ, LANES), F32),
            pltpu.VMEM((TQ, 2 * LANES), F32),
            pltpu.VMEM((HEADS_PER_STEP * TQ, LANES), F32),
            pltpu.VMEM((HEADS_PER_STEP * TQ, LANES), F32),
        ],
        compiler_params=pltpu.CompilerParams(
            dimension_semantics=("arbitrary", "arbitrary", "arbitrary"),
            vmem_limit_bytes=VMEM_LIMIT),
    )(stats, q, cq, k, ck, v)


def _post_kernel(h_ref, mix_ref, qm_ref, mkv_ref, wo_ref, g2_ref, w1_ref, w2_ref, gf_ref, o_ref,
                 *, final_norm):
    qm = qm_ref[...]
    mem_k = mkv_ref[0, :, :MEM_WIDTH]
    mem_v = mkv_ref[0, :, MEM_WIDTH:]
    lane = lax.broadcasted_iota(jnp.int32, (1, MEM_WIDTH), 1)
    zero = jnp.zeros_like(qm)
    mem_out = jnp.zeros(qm.shape, F32)
    for hd in range(N_MEM_HEADS):
        head = (lane >= hd * HEAD_DIM) & (lane < (hd + 1) * HEAD_DIM)
        s = _dot_nt(jnp.where(head, qm, zero), mem_k)
        p = jnp.exp(s - jnp.max(s, axis=1, keepdims=True))
        p = p / jnp.sum(p, axis=1, keepdims=True)
        mem_out = jnp.where(head, _dot(p.astype(BF16), mem_v), mem_out)

    merged = jnp.concatenate([mix_ref[...], mem_out.astype(BF16)], axis=1)
    h1 = h_ref[...] + _dot(merged, wo_ref[...])

    xn = _rms_norm_rows(h1, g2_ref[...]).astype(BF16)
    mlp = jnp.zeros(h1.shape, F32)
    for c0 in range(0, D_FF, FF_TILE):
        a = jnp.maximum(_dot(xn, w1_ref[:, c0:c0 + FF_TILE]), 0.0)
        mlp = mlp + _dot((a * a).astype(BF16), w2_ref[c0:c0 + FF_TILE, :])
    h2 = h1 + mlp
    if final_norm:
        h2 = _rms_norm_rows(h2, gf_ref[...])
    o_ref[...] = h2


def _post(h_rows, mix, qm, mkv_l, wo, g2, w1, w2, gf, batch, final_norm):
    rows = h_rows.shape[0]
    tiles_per_seq = rows // batch // ROW_TILE
    n_mem = mkv_l.shape[1]
    row_spec = lambda wd: pl.BlockSpec((ROW_TILE, wd), lambda i: (i, 0))
    single = pl.Buffered(1)
    const = lambda shape: pl.BlockSpec(shape, lambda i: (0,) * len(shape), pipeline_mode=single)
    return pl.pallas_call(
        functools.partial(_post_kernel, final_norm=final_norm),
        grid=(rows // ROW_TILE,),
        in_specs=[
            row_spec(D_MODEL),
            row_spec(MIX_WIDTH),
            row_spec(MEM_WIDTH),
            pl.BlockSpec((1, n_mem, 2 * MEM_WIDTH), lambda i: (i // tiles_per_seq, 0, 0)),
            const(wo.shape),
            const((1, D_MODEL)),
            const(w1.shape),
            const(w2.shape),
            const((1, D_MODEL)),
        ],
        out_specs=row_spec(D_MODEL),
        out_shape=jax.ShapeDtypeStruct(h_rows.shape, F32),
        compiler_params=pltpu.CompilerParams(
            dimension_semantics=("arbitrary",), vmem_limit_bytes=VMEM_LIMIT),
    )(h_rows, mix, qm, mkv_l, wo, g2.reshape(1, D_MODEL), w1, w2, gf.reshape(1, D_MODEL))


def kernel(x, mem, norm1_g, w_in_a, w_in_b, w_mem_kv, mem_norm_g, w_o, norm2_g, w_mlp1, w_mlp2,
           kv_norm_g, w_kv_shared, b_f, final_norm_g):
    batch, seq_len, _ = x.shape
    n_mem = mem.shape[1]
    depth = norm1_g.shape[0]
    n_a = w_in_a.shape[0]
    assert seq_len % ROW_TILE == 0 and seq_len % TQ == 0 and TQ % TK == 0
    scale = HEAD_DIM ** -0.5

    q_scale_a = jnp.concatenate([jnp.full((MIX_WIDTH,), scale, F32), jnp.ones((2 * MIX_WIDTH,), F32),
                                 jnp.full((MEM_WIDTH,), scale, F32)])
    w_a = (w_in_a * q_scale_a).astype(BF16)
    w_b = (w_in_b * scale).astype(BF16)
    w_mkv = w_mem_kv.astype(BF16)
    w_out = w_o.astype(BF16)
    w1 = w_mlp1.astype(BF16)
    w2 = w_mlp2.astype(BF16)
    pad = LANES - N_MIX_HEADS
    w_kv = jnp.pad(w_kv_shared, ((0, 0), (0, pad))).astype(BF16)
    bf_pad = jnp.pad(b_f.astype(F32), (0, pad)).reshape(1, LANES)

    h = x.reshape(batch * seq_len, D_MODEL)
    mkv = _mem_kv(mem.reshape(batch * n_mem, D_MODEL), mem_norm_g, w_mkv)
    mkv = mkv.reshape(depth * batch, n_mem, 2 * MEM_WIDTH)

    k_sh = v_sh = cq = ck = stats = None
    for l in range(depth):
        if l == n_a:
            k_sh, v_sh, cq, ck, stats = _shared_kv(h, kv_norm_g, w_kv, bf_pad, seq_len)
            stats = stats.reshape(batch, seq_len // ROW_TILE, STATS_ROWS, LANES)[:, :, :N_STATS, :N_MIX_HEADS]
            stats = stats.reshape(batch, seq_len // ROW_TILE, N_STATS * N_MIX_HEADS)
        if l < n_a:
            q, k, v, qm = _norm_proj(h, norm1_g[l], w_a[l],
                                     (MIX_WIDTH, MIX_WIDTH, MIX_WIDTH, MEM_WIDTH),
                                     scales=(LOG2E, None, None, None))
            mix = _stick_attention(q, k, v, batch, seq_len)
        else:
            q, qm = _norm_proj(h, norm1_g[l], w_b[l - n_a], (MIX_WIDTH, MEM_WIDTH))
            mix = _fox_attention(q, cq, k_sh, ck, v_sh, stats, batch, seq_len)
        h = _post(h, mix, qm, mkv[l * batch:(l + 1) * batch], w_out[l], norm2_g[l], w1[l], w2[l],
                  final_norm_g, batch, final_norm=(l == depth - 1))
    return h.reshape(batch, seq_len, D_MODEL)
```

```python
import functools

import jax
import jax.numpy as jnp
import numpy as np
from jax import lax
from jax.experimental import pallas as pl
from jax.experimental.pallas import tpu as pltpu

D_MODEL = 1024
HEAD_DIM = 64
N_MIX_HEADS = 8
N_MEM_HEADS = 4
MIX_WIDTH = N_MIX_HEADS * HEAD_DIM
MEM_WIDTH = N_MEM_HEADS * HEAD_DIM
D_FF = 4 * D_MODEL
EPS = 1e-6
NEG_INF = -1e30
LOG2E = 1.4426950408889634
STICK_SPENT_LOG2 = 160.0
FOX_NEGLIGIBLE = 110.0
NORM_MARGIN = 1.01
STATS_ROWS = 8
N_STATS = 2

LANES = 128
HEADS_PER_STEP = LANES // HEAD_DIM
N_PAIRS = N_MIX_HEADS // HEADS_PER_STEP
GATE_LANES = 8
N_PIECES = 3

ROW_TILE = 512
TQ = 512
TK = 256
FF_TILE = 1024
VMEM_LIMIT = 56 * 1024 * 1024

BF16 = jnp.bfloat16
F32 = jnp.float32


def _dot(a, b):
    return jnp.dot(a, b, preferred_element_type=F32)


def _dot_nt(a, b):
    return lax.dot_general(a, b, (((1,), (1,)), ((), ())), preferred_element_type=F32)


def _rms_norm_rows(x, g):
    return x * lax.rsqrt(jnp.mean(x * x, axis=-1, keepdims=True) + EPS) * g


def _const_spec(shape):
    return pl.BlockSpec(shape, lambda *_: (0,) * len(shape))


def _slot_pair(shape, dtype):
    return (pltpu.VMEM(shape, dtype), pltpu.VMEM(shape, dtype))


def _norm_proj_kernel(x_ref, g_ref, w_ref, *o_refs, scales):
    xn = _rms_norm_rows(x_ref[...], g_ref[...]).astype(BF16)
    y = _dot(xn, w_ref[...])
    off = 0
    for o_ref, scale in zip(o_refs, scales):
        width = o_ref.shape[-1]
        part = y[:, off:off + width]
        o_ref[...] = (part if scale is None else part * scale).astype(o_ref.dtype)
        off += width


def _norm_proj(x, g, w, widths, scales=None):
    rows = x.shape[0]
    n = w.shape[1]
    assert sum(widths) == n and rows % ROW_TILE == 0
    scales = (None,) * len(widths) if scales is None else scales
    return pl.pallas_call(
        functools.partial(_norm_proj_kernel, scales=scales),
        grid=(rows // ROW_TILE,),
        in_specs=[
            pl.BlockSpec((ROW_TILE, D_MODEL), lambda i: (i, 0)),
            _const_spec((1, D_MODEL)),
            _const_spec((D_MODEL, n)),
        ],
        out_specs=[pl.BlockSpec((ROW_TILE, wd), lambda i: (i, 0)) for wd in widths],
        out_shape=[jax.ShapeDtypeStruct((rows, wd), BF16) for wd in widths],
        compiler_params=pltpu.CompilerParams(
            dimension_semantics=("arbitrary",), vmem_limit_bytes=VMEM_LIMIT),
    )(x, g.reshape(1, D_MODEL), w)


def _mem_kv_kernel(x_ref, g_ref, w_ref, o_ref):
    xn = _rms_norm_rows(x_ref[...], g_ref[0]).astype(BF16)
    o_ref[0] = _dot(xn, w_ref[0]).astype(o_ref.dtype)


def _mem_kv(mem_rows, gains, w):
    depth = w.shape[0]
    rows = mem_rows.shape[0]
    return pl.pallas_call(
        _mem_kv_kernel,
        grid=(depth,),
        in_specs=[
            _const_spec((rows, D_MODEL)),
            pl.BlockSpec((1, 1, D_MODEL), lambda l: (l, 0, 0)),
            pl.BlockSpec((1, D_MODEL, 2 * MEM_WIDTH), lambda l: (l, 0, 0)),
        ],
        out_specs=pl.BlockSpec((1, rows, 2 * MEM_WIDTH), lambda l: (l, 0, 0)),
        out_shape=jax.ShapeDtypeStruct((depth, rows, 2 * MEM_WIDTH), BF16),
        compiler_params=pltpu.CompilerParams(
            dimension_semantics=("arbitrary",), vmem_limit_bytes=VMEM_LIMIT),
    )(mem_rows, gains.reshape(depth, 1, D_MODEL), w)


def _split3(c):
    hi = c.astype(BF16)
    r1 = c - hi.astype(F32)
    mid = r1.astype(BF16)
    lo = (r1 - mid.astype(F32)).astype(BF16)
    return hi, mid, lo


def _shared_kv_kernel(x_ref, g_ref, w_ref, bf_ref, eq_ref, ek_ref, oneq_ref, onek_ref, ehead_ref,
                      k_ref, v_ref, cq_ref, ck_ref, stats_ref, carry_ref, knorm_ref, *, tiles_per_seq):
    i = pl.program_id(0)

    @pl.when(i % tiles_per_seq == 0)
    def _():
        carry_ref[...] = jnp.zeros_like(carry_ref)
        knorm_ref[...] = jnp.zeros_like(knorm_ref)

    xn = _rms_norm_rows(x_ref[...], g_ref[...]).astype(BF16)
    y = _dot(xn, w_ref[...])
    k_bf = y[:, :MIX_WIDTH].astype(BF16)
    k_ref[...] = k_bf
    v_ref[...] = y[:, MIX_WIDTH:2 * MIX_WIDTH].astype(BF16)

    k_f = k_bf.astype(F32)
    k_sq = _dot((k_f * k_f).astype(BF16), ehead_ref[...])
    knorm_ref[...] = jnp.maximum(knorm_ref[...], jnp.max(k_sq, axis=0, keepdims=True))

    f = y[:, 2 * MIX_WIDTH:] + bf_ref[...]
    log_f = jnp.minimum(f, 0.0) - jnp.log1p(jnp.exp(-jnp.abs(f)))
    lane = lax.broadcasted_iota(jnp.int32, log_f.shape, 1)
    log_f = jnp.where(lane < N_MIX_HEADS, log_f, 0.0)

    rows = log_f.shape[0]
    r = lax.broadcasted_iota(jnp.int32, (rows, rows), 0)
    c = lax.broadcasted_iota(jnp.int32, (rows, rows), 1)
    tri = (c <= r).astype(BF16)
    hi, mid, lo = _split3(log_f)
    cum = (_dot(tri, hi) + _dot(tri, mid)) + _dot(tri, lo) + carry_ref[...]
    carry_ref[...] = cum[rows - 1:rows, :]

    stats_ref[...] = jnp.concatenate(
        [cum[rows - 1:rows, :], jnp.sqrt(knorm_ref[...]) * NORM_MARGIN,
         jnp.zeros((STATS_ROWS - N_STATS, LANES), F32)], axis=0)

    pieces = jnp.concatenate(_split3(cum), axis=1)
    cq_ref[...] = (_dot(pieces, eq_ref[...]) + oneq_ref[...]).astype(BF16)
    ck_ref[...] = (onek_ref[...] - _dot(pieces, ek_ref[...])).astype(BF16)


def _gate_scatter_constants():
    eq = np.zeros((N_PIECES * LANES, LANES), np.float32)
    ek = np.zeros((N_PIECES * LANES, LANES), np.float32)
    oneq = np.zeros((1, LANES), np.float32)
    onek = np.zeros((1, LANES), np.float32)
    for h in range(N_MIX_HEADS):
        for p in range(N_PIECES):
            eq[p * LANES + h, GATE_LANES * h + p] = 1.0
            ek[p * LANES + h, GATE_LANES * h + N_PIECES + p] = 1.0
            oneq[0, GATE_LANES * h + N_PIECES + p] = 1.0
            onek[0, GATE_LANES * h + p] = 1.0
    return jnp.asarray(eq, BF16), jnp.asarray(ek, BF16), jnp.asarray(oneq), jnp.asarray(onek)


def _head_indicator():
    lane = np.arange(MIX_WIDTH)[:, None] // HEAD_DIM
    return jnp.asarray(lane == np.arange(LANES)[None, :], BF16)


def _shared_kv(h_rows, g, w_pad, bf_pad, seq_len):
    rows = h_rows.shape[0]
    n = w_pad.shape[1]
    eq, ek, oneq, onek = _gate_scatter_constants()
    ehead = _head_indicator()
    kern = functools.partial(_shared_kv_kernel, tiles_per_seq=seq_len // ROW_TILE)
    row_spec = lambda wd: pl.BlockSpec((ROW_TILE, wd), lambda i: (i, 0))
    return pl.pallas_call(
        kern,
        grid=(rows // ROW_TILE,),
        in_specs=[
            row_spec(D_MODEL),
            _const_spec((1, D_MODEL)),
            _const_spec((D_MODEL, n)),
            _const_spec((1, LANES)),
            _const_spec(eq.shape),
            _const_spec(ek.shape),
            _const_spec((1, LANES)),
            _const_spec((1, LANES)),
            _const_spec(ehead.shape),
        ],
        out_specs=[row_spec(MIX_WIDTH), row_spec(MIX_WIDTH), row_spec(LANES), row_spec(LANES),
                   pl.BlockSpec((STATS_ROWS, LANES), lambda i: (i, 0))],
        out_shape=[
            jax.ShapeDtypeStruct((rows, MIX_WIDTH), BF16),
            jax.ShapeDtypeStruct((rows, MIX_WIDTH), BF16),
            jax.ShapeDtypeStruct((rows, LANES), BF16),
            jax.ShapeDtypeStruct((rows, LANES), BF16),
            jax.ShapeDtypeStruct((rows // ROW_TILE * STATS_ROWS, LANES), F32),
        ],
        scratch_shapes=[pltpu.VMEM((1, LANES), F32), pltpu.VMEM((1, LANES), F32)],
        compiler_params=pltpu.CompilerParams(
            dimension_semantics=("arbitrary",), vmem_limit_bytes=VMEM_LIMIT),
    )(h_rows, g.reshape(1, D_MODEL), w_pad, bf_pad, eq, ek, oneq, onek, ehead)


def _head_lane_masks():
    lane = lax.broadcasted_iota(jnp.int32, (1, LANES), 1)
    return [(lane >= h * HEAD_DIM) & (lane < (h + 1) * HEAD_DIM) for h in range(HEADS_PER_STEP)]


def _stack_heads(x2, masks):
    zero = jnp.zeros_like(x2)
    return jnp.concatenate([jnp.where(m, x2, zero) for m in masks], axis=0)


def _heads_to_lanes(x):
    n = x.shape[0] // HEADS_PER_STEP
    return jnp.concatenate([x[h * n:(h + 1) * n] for h in range(HEADS_PER_STEP)], axis=1)


def _visible(t0, start, width, strict):
    t_pos = t0 + (lax.broadcasted_iota(jnp.int32, (HEADS_PER_STEP * TQ, width), 0) & (TQ - 1))
    s_pos = start + lax.broadcasted_iota(jnp.int32, (HEADS_PER_STEP * TQ, width), 1)
    return s_pos < t_pos if strict else s_pos <= t_pos


def _stick_kernel(q_ref, k_ref, v_ref, o_ref, q_scr, tri_scr, z_scr, zl_scr, w_scr, later_scr, acc_scr):
    t0 = pl.program_id(2) * TQ
    masks = _head_lane_masks()
    q_scr[...] = _stack_heads(q_ref[...], masks)
    r = lax.broadcasted_iota(jnp.int32, (TK, TK), 0)
    c = lax.broadcasted_iota(jnp.int32, (TK, TK), 1)
    tri_scr[...] = (r >= c).astype(BF16)
    later_scr[...] = jnp.zeros_like(later_scr)
    acc_scr[...] = jnp.zeros_like(acc_scr)

    def key_start(n):
        return pl.multiple_of(jnp.maximum(t0 + (TQ // TK - 1 - n) * TK, 0), TK)

    def visible(n):
        return _visible(t0, t0 + (TQ // TK - 1 - n) * TK, TK, strict=True)

    def scores(n, slot):
        z_scr[slot][...] = _dot_nt(q_scr[...], k_ref[pl.ds(key_start(n), TK), :])

    def weights(slot, mask=None):
        z = z_scr[slot][...]
        sp = jnp.maximum(z, 0.0) + jnp.log(1.0 + jnp.exp2(-jnp.abs(z))) * LOG2E
        if mask is not None:
            sp = jnp.where(mask, sp, 0.0)
        zl_scr[...] = z - later_scr[...]
        within = _dot(sp.astype(BF16), tri_scr[...])
        w = jnp.exp2(zl_scr[...] - within)
        if mask is not None:
            w = jnp.where(mask, w, 0.0)
        w_scr[slot][...] = _heads_to_lanes(w.astype(BF16))
        later_scr[...] += within[:, 0:1]

    def values(n, slot):
        acc_scr[...] += _dot(w_scr[slot][...], _stack_heads(v_ref[pl.ds(key_start(n), TK), :], masks))

    assert TQ == 2 * TK
    n_blocks = TQ // TK + t0 // TK
    scores(0, 0)
    weights(0, visible(0)); scores(1, 1)
    values(0, 0); weights(1, visible(1)); scores(2, 0)

    def more(carry):
        n, spent = carry
        return jnp.logical_and(n < n_blocks - 2, jnp.logical_not(spent))

    def stick_spent():
        return jnp.min(later_scr[...]) >= STICK_SPENT_LOG2

    def trip(carry):
        n, _ = carry
        scores(n + 2, 1); values(n, 1); weights(0)
        spent_early = stick_spent()

        @pl.when(jnp.logical_not(spent_early))
        def _():
            scores(n + 3, 0); values(n + 1, 0); weights(1)

        return jnp.where(spent_early, n + 1, n + 2), stick_spent()

    n_end, _ = lax.while_loop(more, trip, (jnp.int32(1), jnp.bool_(False)))

    @pl.when(n_end % 2 == 1)
    def _():
        values(n_end, 1)

    @pl.when(n_end % 2 == 0)
    def _():
        values(n_end, 0)

    o_ref[...] = acc_scr[...].astype(o_ref.dtype)


def _stick_attention(q, k, v, batch, seq_len):
    n_q = seq_len // TQ
    q_spec = pl.BlockSpec((TQ, LANES), lambda b, p, i: (b * n_q + i, p))
    kv_spec = pl.BlockSpec((seq_len, LANES), lambda b, p, i: (b, p))
    return pl.pallas_call(
        _stick_kernel,
        grid=(batch, N_PAIRS, n_q),
        in_specs=[q_spec, kv_spec, kv_spec],
        out_specs=q_spec,
        out_shape=jax.ShapeDtypeStruct(q.shape, BF16),
        scratch_shapes=[
            pltpu.VMEM((HEADS_PER_STEP * TQ, LANES), BF16),
            pltpu.VMEM((TK, TK), BF16),
            _slot_pair((HEADS_PER_STEP * TQ, TK), F32),
            pltpu.VMEM((HEADS_PER_STEP * TQ, TK), F32),
            _slot_pair((TQ, HEADS_PER_STEP * TK), BF16),
            pltpu.VMEM((HEADS_PER_STEP * TQ, 1), F32),
            pltpu.VMEM((TQ, LANES), F32),
        ],
        compiler_params=pltpu.CompilerParams(
            dimension_semantics=("arbitrary", "arbitrary", "arbitrary"),
            vmem_limit_bytes=VMEM_LIMIT),
    )(q, k, v)


def _fox_kernel(stats_ref, q_ref, cq_ref, k_ref, ck_ref, v_ref, o_ref,
                q_scr, ones_scr, z_scr, p_scr, alpha_scr, m_scr, acc_scr, qnorm_scr, crow_scr):
    batch_idx = pl.program_id(0)
    pair = pl.program_id(1)
    t0 = pl.program_id(2) * TQ
    masks = _head_lane_masks()
    lane = lax.broadcasted_iota(jnp.int32, (1, LANES), 1)
    cq = cq_ref[...]
    gate_rows = []
    for h in range(HEADS_PER_STEP):
        g0 = (pair * HEADS_PER_STEP + h) * GATE_LANES
        gate_rows.append(jnp.where((lane >= g0) & (lane < g0 + GATE_LANES), cq, jnp.zeros_like(cq)))
    q_rows = _stack_heads(q_ref[...], masks)
    gates = jnp.concatenate(gate_rows, axis=0)
    q_scr[...] = jnp.concatenate([q_rows, gates], axis=1)
    lane_rows = lax.broadcasted_iota(jnp.int32, (HEADS_PER_STEP * TK, LANES), 1) // HEAD_DIM
    head_rows = lax.broadcasted_iota(jnp.int32, (HEADS_PER_STEP * TK, LANES), 0) // TK
    ones_scr[...] = (lane_rows == head_rows).astype(F32).astype(BF16)
    m_scr[...] = jnp.full_like(m_scr, NEG_INF)
    acc_scr[...] = jnp.zeros_like(acc_scr)

    q_f = q_rows.astype(F32)
    q_norm = jnp.sqrt(jnp.sum(q_f * q_f, axis=1, keepdims=True)) * NORM_MARGIN
    qnorm_scr[...] = jnp.broadcast_to(q_norm, qnorm_scr.shape)
    piece_lanes = (lane % GATE_LANES) < N_PIECES
    c_row = jnp.sum(jnp.where(piece_lanes, gates.astype(F32), 0.0), axis=1, keepdims=True)
    crow_scr[...] = jnp.broadcast_to(c_row, crow_scr.shape)

    def key_start(n):
        back = jnp.maximum(t0 - (n - 1) * TK, 0)
        return pl.multiple_of(jnp.where(n < 2, t0 + n * TK, back), TK)

    def visible(n):
        return _visible(t0, t0 + n * TK, TK, strict=False)

    def scores(n, slot):
        start = key_start(n)
        k_blk = jnp.concatenate([k_ref[pl.ds(start, TK), :], ck_ref[pl.ds(start, TK), :]], axis=1)
        z_scr[slot][...] = _dot_nt(q_scr[...], k_blk)

    def probs(slot, mask=None):
        z = z_scr[slot][...]
        if mask is not None:
            z = jnp.where(mask, z, NEG_INF)
        m = m_scr[...]
        m_new = jnp.maximum(m, jnp.max(z, axis=1, keepdims=True))
        m_scr[...] = m_new
        shift = jnp.concatenate([m_new] * (TK // LANES), axis=1)
        p_scr[slot][...] = _heads_to_lanes(jnp.exp(z - shift).astype(BF16))
        alpha = jnp.exp(m - m_new)
        alpha_scr[slot][...] = jnp.where(masks[0], alpha[:TQ], alpha[TQ:])

    def accumulate(n, slot):
        v_rows = jnp.concatenate(
            [_stack_heads(v_ref[pl.ds(key_start(n), TK), :], masks), ones_scr[...]], axis=1)
        alpha = alpha_scr[slot][...]
        acc_scr[...] = (jnp.concatenate([alpha, alpha], axis=1) * acc_scr[...]
                        + _dot(p_scr[slot][...], v_rows))

    def rest_is_negligible(n_done):
        last_key = key_start(n_done) - 1
        tile = jnp.maximum(last_key, 0) // ROW_TILE
        worst = None
        for h in range(HEADS_PER_STEP):
            head = pair * HEADS_PER_STEP + h
            c_floor = stats_ref[batch_idx, tile, head]
            k_norm = stats_ref[batch_idx, tile, N_MIX_HEADS + head]
            rows = slice(h * TQ, (h + 1) * TQ)
            gap = (qnorm_scr[rows, :] * k_norm + (crow_scr[rows, :] - c_floor)
                   - m_scr[rows, :] + FOX_NEGLIGIBLE)
            top = jnp.max(gap)
            worst = top if worst is None else jnp.maximum(worst, top)
        return worst <= 0.0

    assert TQ == 2 * TK
    n_blocks = TQ // TK + t0 // TK
    scores(0, 0)
    probs(0, visible(0)); scores(1, 1)
    accumulate(0, 0); probs(1, visible(1)); scores(2, 0)

    def more(carry):
        n, done = carry
        return jnp.logical_and(n < n_blocks - 2, jnp.logical_not(done))

    def trip(carry):
        n, _ = carry
        scores(n + 2, 1); accumulate(n, 1); probs(0)
        scores(n + 3, 0); accumulate(n + 1, 0); probs(1)
        return n + 2, rest_is_negligible(n + 2)

    n_end, _ = lax.while_loop(more, trip, (jnp.int32(1), jnp.bool_(False)))
    accumulate(n_end, 1)
    acc = acc_scr[...]
    o_ref[...] = (acc[:, :LANES] / acc[:, LANES:]).astype(o_ref.dtype)


def _fox_attention(q, cq, k, ck, v, stats, batch, seq_len):
    n_q = seq_len // TQ
    q_spec = pl.BlockSpec((TQ, LANES), lambda b, p, i: (b * n_q + i, p))
    cq_spec = pl.BlockSpec((TQ, LANES), lambda b, p, i: (b * n_q + i, 0))
    kv_spec = pl.BlockSpec((seq_len, LANES), lambda b, p, i: (b, p))
    ck_spec = pl.BlockSpec((seq_len, LANES), lambda b, p, i: (b, 0))
    return pl.pallas_call(
        _fox_kernel,
        grid=(batch, N_PAIRS, n_q),
        in_specs=[pl.BlockSpec(memory_space=pltpu.SMEM), q_spec, cq_spec, kv_spec, ck_spec, kv_spec],
        out_specs=q_spec,
        out_shape=jax.ShapeDtypeStruct(q.shape, BF16),
        scratch_shapes=[
            pltpu.VMEM((HEADS_PER_STEP * TQ, 2 * LANES), BF16),
            pltpu.VMEM((HEADS_PER_STEP * TK, LANES), BF16),
            _slot_pair((HEADS_PER_STEP * TQ, TK), F32),
            _slot_pair((TQ, HEADS_PER_STEP * TK), BF16),
            _slot_pair((TQ, LANES), F32),
            pltpu.VMEM((HEADS_PER_STEP * TQ, LANES), F32),
            pltpu.VMEM((TQ, 2 * LANES), F32),
            pltpu.VMEM((HEADS_PER_STEP * TQ, LANES), F32),
            pltpu.VMEM((HEADS_PER_STEP * TQ, LANES), F32),
        ],
        compiler_params=pltpu.CompilerParams(
            dimension_semantics=("arbitrary", "arbitrary", "arbitrary"),
            vmem_limit_bytes=VMEM_LIMIT),
    )(stats, q, cq, k, ck, v)


def _post_kernel(h_ref, mix_ref, qm_ref, mkv_ref, wo_ref, g2_ref, w1_ref, w2_ref, gf_ref, o_ref,
                 *, final_norm):
    qm = qm_ref[...]
    mem_k = mkv_ref[0, :, :MEM_WIDTH]
    mem_v = mkv_ref[0, :, MEM_WIDTH:]
    lane = lax.broadcasted_iota(jnp.int32, (1, MEM_WIDTH), 1)
    zero = jnp.zeros_like(qm)
    mem_out = jnp.zeros(qm.shape, F32)
    for hd in range(N_MEM_HEADS):
        head = (lane >= hd * HEAD_DIM) & (lane < (hd + 1) * HEAD_DIM)
        s = _dot_nt(jnp.where(head, qm, zero), mem_k)
        p = jnp.exp(s - jnp.max(s, axis=1, keepdims=True))
        p = p / jnp.sum(p, axis=1, keepdims=True)
        mem_out = jnp.where(head, _dot(p.astype(BF16), mem_v), mem_out)

    merged = jnp.concatenate([mix_ref[...], mem_out.astype(BF16)], axis=1)
    h1 = h_ref[...] + _dot(merged, wo_ref[...])

    xn = _rms_norm_rows(h1, g2_ref[...]).astype(BF16)
    mlp = jnp.zeros(h1.shape, F32)
    for c0 in range(0, D_FF, FF_TILE):
        a = jnp.maximum(_dot(xn, w1_ref[:, c0:c0 + FF_TILE]), 0.0)
        mlp = mlp + _dot((a * a).astype(BF16), w2_ref[c0:c0 + FF_TILE, :])
    h2 = h1 + mlp
    if final_norm:
        h2 = _rms_norm_rows(h2, gf_ref[...])
    o_ref[...] = h2


def _post(h_rows, mix, qm, mkv_l, wo, g2, w1, w2, gf, batch, final_norm):
    rows = h_rows.shape[0]
    tiles_per_seq = rows // batch // ROW_TILE
    n_mem = mkv_l.shape[1]
    row_spec = lambda wd: pl.BlockSpec((ROW_TILE, wd), lambda i: (i, 0))
    single = pl.Buffered(1)
    const = lambda shape: pl.BlockSpec(shape, lambda i: (0,) * len(shape), pipeline_mode=single)
    return pl.pallas_call(
        functools.partial(_post_kernel, final_norm=final_norm),
        grid=(rows // ROW_TILE,),
        in_specs=[
            row_spec(D_MODEL),
            row_spec(MIX_WIDTH),
            row_spec(MEM_WIDTH),
            pl.BlockSpec((1, n_mem, 2 * MEM_WIDTH), lambda i: (i // tiles_per_seq, 0, 0)),
            const(wo.shape),
            const((1, D_MODEL)),
            const(w1.shape),
            const(w2.shape),
            const((1, D_MODEL)),
        ],
        out_specs=row_spec(D_MODEL),
        out_shape=jax.ShapeDtypeStruct(h_rows.shape, F32),
        compiler_params=pltpu.CompilerParams(
            dimension_semantics=("arbitrary",), vmem_limit_bytes=VMEM_LIMIT),
    )(h_rows, mix, qm, mkv_l, wo, g2.reshape(1, D_MODEL), w1, w2, gf.reshape(1, D_MODEL))


def kernel(x, mem, norm1_g, w_in_a, w_in_b, w_mem_kv, mem_norm_g, w_o, norm2_g, w_mlp1, w_mlp2,
           kv_norm_g, w_kv_shared, b_f, final_norm_g):
    batch, seq_len, _ = x.shape
    n_mem = mem.shape[1]
    depth = norm1_g.shape[0]
    n_a = w_in_a.shape[0]
    assert seq_len % ROW_TILE == 0 and seq_len % TQ == 0 and TQ % TK == 0
    scale = HEAD_DIM ** -0.5

    q_scale_a = jnp.concatenate([jnp.full((MIX_WIDTH,), scale, F32), jnp.ones((2 * MIX_WIDTH,), F32),
                                 jnp.full((MEM_WIDTH,), scale, F32)])
    w_a = (w_in_a * q_scale_a).astype(BF16)
    w_b = (w_in_b * scale).astype(BF16)
    w_mkv = w_mem_kv.astype(BF16)
    w_out = w_o.astype(BF16)
    w1 = w_mlp1.astype(BF16)
    w2 = w_mlp2.astype(BF16)
    pad = LANES - N_MIX_HEADS
    w_kv = jnp.pad(w_kv_shared, ((0, 0), (0, pad))).astype(BF16)
    bf_pad = jnp.pad(b_f.astype(F32), (0, pad)).reshape(1, LANES)

    h = x.reshape(batch * seq_len, D_MODEL)
    mkv = _mem_kv(mem.reshape(batch * n_mem, D_MODEL), mem_norm_g, w_mkv)
    mkv = mkv.reshape(depth * batch, n_mem, 2 * MEM_WIDTH)

    k_sh = v_sh = cq = ck = stats = None
    for l in range(depth):
        if l == n_a:
            k_sh, v_sh, cq, ck, stats = _shared_kv(h, kv_norm_g, w_kv, bf_pad, seq_len)
            stats = stats.reshape(batch, seq_len // ROW_TILE, STATS_ROWS, LANES)[:, :, :N_STATS, :N_MIX_HEADS]
            stats = stats.reshape(batch, seq_len // ROW_TILE, N_STATS * N_MIX_HEADS)
        if l < n_a:
            q, k, v, qm = _norm_proj(h, norm1_g[l], w_a[l],
                                     (MIX_WIDTH, MIX_WIDTH, MIX_WIDTH, MEM_WIDTH),
                                     scales=(LOG2E, None, None, None))
            mix = _stick_attention(q, k, v, batch, seq_len)
        else:
            q, qm = _norm_proj(h, norm1_g[l], w_b[l - n_a], (MIX_WIDTH, MEM_WIDTH))
            mix = _fox_attention(q, cq, k_sh, ck, v_sh, stats, batch, seq_len)
        h = _post(h, mix, qm, mkv[l * batch:(l + 1) * batch], w_out[l], norm2_g[l], w1[l], w2[l],
                  final_norm_g, batch, final_norm=(l == depth - 1))
    return h.reshape(batch, seq_len, D_MODEL)
```

```python
import functools

import jax
import jax.numpy as jnp
import numpy as np
from jax import lax
from jax.experimental import pallas as pl
from jax.experimental.pallas import tpu as pltpu

D_MODEL = 1024
HEAD_DIM = 64
N_MIX_HEADS = 8
N_MEM_HEADS = 4
MIX_WIDTH = N_MIX_HEADS * HEAD_DIM
MEM_WIDTH = N_MEM_HEADS * HEAD_DIM
D_FF = 4 * D_MODEL
EPS = 1e-6
NEG_INF = -1e30
LOG2E = 1.4426950408889634
STICK_SPENT_LOG2 = 160.0
FOX_NEGLIGIBLE = 110.0
NORM_MARGIN = 1.01
STATS_ROWS = 8
N_STATS = 2

LANES = 128
HEADS_PER_STEP = LANES // HEAD_DIM
N_PAIRS = N_MIX_HEADS // HEADS_PER_STEP
PAIRS_PER_STEP = 2
GATE_LANES = 8
N_PIECES = 3

ROW_TILE = 512
TQ = 512
TK = 256
FF_TILE = 1024
VMEM_LIMIT = 56 * 1024 * 1024

BF16 = jnp.bfloat16
F32 = jnp.float32


def _dot(a, b):
    return jnp.dot(a, b, preferred_element_type=F32)


def _dot_nt(a, b):
    return lax.dot_general(a, b, (((1,), (1,)), ((), ())), preferred_element_type=F32)


def _rms_norm_rows(x, g):
    return x * lax.rsqrt(jnp.mean(x * x, axis=-1, keepdims=True) + EPS) * g


def _const_spec(shape):
    return pl.BlockSpec(shape, lambda *_: (0,) * len(shape))


def _slot_pair(shape, dtype):
    return (pltpu.VMEM(shape, dtype), pltpu.VMEM(shape, dtype))


def _norm_proj_kernel(x_ref, g_ref, w_ref, *o_refs, scales):
    xn = _rms_norm_rows(x_ref[...], g_ref[...]).astype(BF16)
    y = _dot(xn, w_ref[...])
    off = 0
    for o_ref, scale in zip(o_refs, scales):
        width = o_ref.shape[-1]
        part = y[:, off:off + width]
        o_ref[...] = (part if scale is None else part * scale).astype(o_ref.dtype)
        off += width


def _norm_proj(x, g, w, widths, scales=None):
    rows = x.shape[0]
    n = w.shape[1]
    assert sum(widths) == n and rows % ROW_TILE == 0
    scales = (None,) * len(widths) if scales is None else scales
    return pl.pallas_call(
        functools.partial(_norm_proj_kernel, scales=scales),
        grid=(rows // ROW_TILE,),
        in_specs=[
            pl.BlockSpec((ROW_TILE, D_MODEL), lambda i: (i, 0)),
            _const_spec((1, D_MODEL)),
            _const_spec((D_MODEL, n)),
        ],
        out_specs=[pl.BlockSpec((ROW_TILE, wd), lambda i: (i, 0)) for wd in widths],
        out_shape=[jax.ShapeDtypeStruct((rows, wd), BF16) for wd in widths],
        compiler_params=pltpu.CompilerParams(
            dimension_semantics=("arbitrary",), vmem_limit_bytes=VMEM_LIMIT),
    )(x, g.reshape(1, D_MODEL), w)


def _mem_kv_kernel(x_ref, g_ref, w_ref, o_ref):
    xn = _rms_norm_rows(x_ref[...], g_ref[0]).astype(BF16)
    o_ref[0] = _dot(xn, w_ref[0]).astype(o_ref.dtype)


def _mem_kv(mem_rows, gains, w):
    depth = w.shape[0]
    rows = mem_rows.shape[0]
    return pl.pallas_call(
        _mem_kv_kernel,
        grid=(depth,),
        in_specs=[
            _const_spec((rows, D_MODEL)),
            pl.BlockSpec((1, 1, D_MODEL), lambda l: (l, 0, 0)),
            pl.BlockSpec((1, D_MODEL, 2 * MEM_WIDTH), lambda l: (l, 0, 0)),
        ],
        out_specs=pl.BlockSpec((1, rows, 2 * MEM_WIDTH), lambda l: (l, 0, 0)),
        out_shape=jax.ShapeDtypeStruct((depth, rows, 2 * MEM_WIDTH), BF16),
        compiler_params=pltpu.CompilerParams(
            dimension_semantics=("arbitrary",), vmem_limit_bytes=VMEM_LIMIT),
    )(mem_rows, gains.reshape(depth, 1, D_MODEL), w)


def _split3(c):
    hi = c.astype(BF16)
    r1 = c - hi.astype(F32)
    mid = r1.astype(BF16)
    lo = (r1 - mid.astype(F32)).astype(BF16)
    return hi, mid, lo


def _shared_kv_kernel(x_ref, g_ref, w_ref, bf_ref, eq_ref, ek_ref, oneq_ref, onek_ref, ehead_ref,
                      k_ref, v_ref, cq_ref, ck_ref, stats_ref, carry_ref, knorm_ref, *, tiles_per_seq):
    i = pl.program_id(0)

    @pl.when(i % tiles_per_seq == 0)
    def _():
        carry_ref[...] = jnp.zeros_like(carry_ref)
        knorm_ref[...] = jnp.zeros_like(knorm_ref)

    xn = _rms_norm_rows(x_ref[...], g_ref[...]).astype(BF16)
    y = _dot(xn, w_ref[...])
    k_bf = y[:, :MIX_WIDTH].astype(BF16)
    k_ref[...] = k_bf
    v_ref[...] = y[:, MIX_WIDTH:2 * MIX_WIDTH].astype(BF16)

    k_f = k_bf.astype(F32)
    k_sq = _dot((k_f * k_f).astype(BF16), ehead_ref[...])
    knorm_ref[...] = jnp.maximum(knorm_ref[...], jnp.max(k_sq, axis=0, keepdims=True))

    f = y[:, 2 * MIX_WIDTH:] + bf_ref[...]
    log_f = jnp.minimum(f, 0.0) - jnp.log1p(jnp.exp(-jnp.abs(f)))
    lane = lax.broadcasted_iota(jnp.int32, log_f.shape, 1)
    log_f = jnp.where(lane < N_MIX_HEADS, log_f, 0.0)

    rows = log_f.shape[0]
    r = lax.broadcasted_iota(jnp.int32, (rows, rows), 0)
    c = lax.broadcasted_iota(jnp.int32, (rows, rows), 1)
    tri = (c <= r).astype(BF16)
    hi, mid, lo = _split3(log_f)
    cum = (_dot(tri, hi) + _dot(tri, mid)) + _dot(tri, lo) + carry_ref[...]
    carry_ref[...] = cum[rows - 1:rows, :]

    stats_ref[...] = jnp.concatenate(
        [cum[rows - 1:rows, :], jnp.sqrt(knorm_ref[...]) * NORM_MARGIN,
         jnp.zeros((STATS_ROWS - N_STATS, LANES), F32)], axis=0)

    pieces = jnp.concatenate(_split3(cum), axis=1)
    cq_ref[...] = (_dot(pieces, eq_ref[...]) + oneq_ref[...]).astype(BF16)
    ck_ref[...] = (onek_ref[...] - _dot(pieces, ek_ref[...])).astype(BF16)


def _gate_scatter_constants():
    eq = np.zeros((N_PIECES * LANES, LANES), np.float32)
    ek = np.zeros((N_PIECES * LANES, LANES), np.float32)
    oneq = np.zeros((1, LANES), np.float32)
    onek = np.zeros((1, LANES), np.float32)
    for h in range(N_MIX_HEADS):
        for p in range(N_PIECES):
            eq[p * LANES + h, GATE_LANES * h + p] = 1.0
            ek[p * LANES + h, GATE_LANES * h + N_PIECES + p] = 1.0
            oneq[0, GATE_LANES * h + N_PIECES + p] = 1.0
            onek[0, GATE_LANES * h + p] = 1.0
    return jnp.asarray(eq, BF16), jnp.asarray(ek, BF16), jnp.asarray(oneq), jnp.asarray(onek)


def _head_indicator():
    lane = np.arange(MIX_WIDTH)[:, None] // HEAD_DIM
    return jnp.asarray(lane == np.arange(LANES)[None, :], BF16)


def _shared_kv(h_rows, g, w_pad, bf_pad, seq_len):
    rows = h_rows.shape[0]
    n = w_pad.shape[1]
    eq, ek, oneq, onek = _gate_scatter_constants()
    ehead = _head_indicator()
    kern = functools.partial(_shared_kv_kernel, tiles_per_seq=seq_len // ROW_TILE)
    row_spec = lambda wd: pl.BlockSpec((ROW_TILE, wd), lambda i: (i, 0))
    return pl.pallas_call(
        kern,
        grid=(rows // ROW_TILE,),
        in_specs=[
            row_spec(D_MODEL),
            _const_spec((1, D_MODEL)),
            _const_spec((D_MODEL, n)),
            _const_spec((1, LANES)),
            _const_spec(eq.shape),
            _const_spec(ek.shape),
            _const_spec((1, LANES)),
            _const_spec((1, LANES)),
            _const_spec(ehead.shape),
        ],
        out_specs=[row_spec(MIX_WIDTH), row_spec(MIX_WIDTH), row_spec(LANES), row_spec(LANES),
                   pl.BlockSpec((STATS_ROWS, LANES), lambda i: (i, 0))],
        out_shape=[
            jax.ShapeDtypeStruct((rows, MIX_WIDTH), BF16),
            jax.ShapeDtypeStruct((rows, MIX_WIDTH), BF16),
            jax.ShapeDtypeStruct((rows, LANES), BF16),
            jax.ShapeDtypeStruct((rows, LANES), BF16),
            jax.ShapeDtypeStruct((rows // ROW_TILE * STATS_ROWS, LANES), F32),
        ],
        scratch_shapes=[pltpu.VMEM((1, LANES), F32), pltpu.VMEM((1, LANES), F32)],
        compiler_params=pltpu.CompilerParams(
            dimension_semantics=("arbitrary",), vmem_limit_bytes=VMEM_LIMIT),
    )(h_rows, g.reshape(1, D_MODEL), w_pad, bf_pad, eq, ek, oneq, onek, ehead)


def _head_lane_masks():
    lane = lax.broadcasted_iota(jnp.int32, (1, LANES), 1)
    return [(lane >= h * HEAD_DIM) & (lane < (h + 1) * HEAD_DIM) for h in range(HEADS_PER_STEP)]


def _stack_heads(x2, masks):
    zero = jnp.zeros_like(x2)
    return jnp.concatenate([jnp.where(m, x2, zero) for m in masks], axis=0)


def _heads_to_lanes(x):
    n = x.shape[0] // HEADS_PER_STEP
    return jnp.concatenate([x[h * n:(h + 1) * n] for h in range(HEADS_PER_STEP)], axis=1)


def _visible(t0, start, width, strict, rows=TQ):
    t_pos = t0 + (lax.broadcasted_iota(jnp.int32, (HEADS_PER_STEP * rows, width), 0) & (rows - 1))
    s_pos = start + lax.broadcasted_iota(jnp.int32, (HEADS_PER_STEP * rows, width), 1)
    return s_pos < t_pos if strict else s_pos <= t_pos


def _stick_pair(t0, lanes, masks, q_ref, k_ref, v_ref, o_ref, tri_scr,
                q_scr, z_scr, zl_scr, w_scr, later_scr, acc_scr):
    q_scr[...] = _stack_heads(q_ref[:, lanes], masks)
    later_scr[...] = jnp.zeros_like(later_scr)
    acc_scr[...] = jnp.zeros_like(acc_scr)

    def key_start(n):
        return pl.multiple_of(jnp.maximum(t0 + (TQ // TK - 1 - n) * TK, 0), TK)

    def visible(n):
        return _visible(t0, t0 + (TQ // TK - 1 - n) * TK, TK, strict=True)

    def scores(n, slot):
        z_scr[slot][...] = _dot_nt(q_scr[...], k_ref[pl.ds(key_start(n), TK), lanes])

    def weights(slot, mask=None):
        z = z_scr[slot][...]
        sp = jnp.maximum(z, 0.0) + jnp.log(1.0 + jnp.exp2(-jnp.abs(z))) * LOG2E
        if mask is not None:
            sp = jnp.where(mask, sp, 0.0)
        zl_scr[...] = z - later_scr[...]
        within = _dot(sp.astype(BF16), tri_scr[...])
        w = jnp.exp2(zl_scr[...] - within)
        if mask is not None:
            w = jnp.where(mask, w, 0.0)
        w_scr[slot][...] = _heads_to_lanes(w.astype(BF16))
        later_scr[...] += within[:, 0:1]

    def values(n, slot):
        acc_scr[...] += _dot(w_scr[slot][...], _stack_heads(v_ref[pl.ds(key_start(n), TK), lanes], masks))

    def latest_block():
        late = lambda x: jnp.concatenate(
            [x[h * TQ + TK:(h + 1) * TQ] for h in range(HEADS_PER_STEP)], axis=0)
        start = key_start(0)
        z = _dot_nt(late(q_scr[...]), k_ref[pl.ds(start, TK), lanes])
        mask = _visible(t0 + TK, start, TK, strict=True, rows=TQ - TK)
        sp = jnp.maximum(z, 0.0) + jnp.log(1.0 + jnp.exp2(-jnp.abs(z))) * LOG2E
        within = _dot(jnp.where(mask, sp, 0.0).astype(BF16), tri_scr[...])
        w = jnp.where(mask, jnp.exp2(z - within), 0.0)
        acc_scr[TK:TQ, :] += _dot(_heads_to_lanes(w.astype(BF16)),
                                  _stack_heads(v_ref[pl.ds(start, TK), lanes], masks))
        for h in range(HEADS_PER_STEP):
            later_scr[h * TQ + TK:(h + 1) * TQ, :] += within[h * (TQ - TK):(h + 1) * (TQ - TK), 0:1]

    assert TQ == 2 * TK
    n_blocks = TQ // TK + t0 // TK
    latest_block(); scores(1, 1)
    weights(1, visible(1)); scores(2, 0)
    yield

    def more(carry):
        n, spent = carry
        return jnp.logical_and(n < n_blocks - 2, jnp.logical_not(spent))

    def stick_spent():
        return jnp.min(later_scr[...]) >= STICK_SPENT_LOG2

    def trip(carry):
        n, _ = carry
        scores(n + 2, 1); values(n, 1); weights(0)
        spent_early = stick_spent()

        @pl.when(jnp.logical_not(spent_early))
        def _():
            scores(n + 3, 0); values(n + 1, 0); weights(1)

        return jnp.where(spent_early, n + 1, n + 2), stick_spent()

    n_end, _ = lax.while_loop(more, trip, (jnp.int32(1), jnp.bool_(False)))

    @pl.when(n_end % 2 == 1)
    def _():
        values(n_end, 1)

    @pl.when(n_end % 2 == 0)
    def _():
        values(n_end, 0)

    o_ref[:, lanes] = acc_scr[...].astype(o_ref.dtype)


def _stick_kernel(q_ref, k_ref, v_ref, o_ref, tri_scr, *pair_scratch):
    t0 = pl.program_id(2) * TQ
    masks = _head_lane_masks()
    r = lax.broadcasted_iota(jnp.int32, (TK, TK), 0)
    c = lax.broadcasted_iota(jnp.int32, (TK, TK), 1)
    tri_scr[...] = (r >= c).astype(BF16)
    pairs = [_stick_pair(t0, slice(p * LANES, (p + 1) * LANES), masks, q_ref, k_ref, v_ref, o_ref,
                         tri_scr, *pair_scratch[p]) for p in range(PAIRS_PER_STEP)]
    for program in pairs:
        next(program)
    for program in pairs:
        next(program, None)


def _stick_attention(q, k, v, batch, seq_len):
    n_q = seq_len // TQ
    width = PAIRS_PER_STEP * LANES
    q_spec = pl.BlockSpec((TQ, width), lambda b, g, i: (b * n_q + i, g))
    kv_spec = pl.BlockSpec((seq_len, width), lambda b, g, i: (b, g))
    pair_scratch = (
        pltpu.VMEM((HEADS_PER_STEP * TQ, LANES), BF16),
        _slot_pair((HEADS_PER_STEP * TQ, TK), F32),
        pltpu.VMEM((HEADS_PER_STEP * TQ, TK), F32),
        _slot_pair((TQ, HEADS_PER_STEP * TK), BF16),
        pltpu.VMEM((HEADS_PER_STEP * TQ, 1), F32),
        pltpu.VMEM((TQ, LANES), F32),
    )
    return pl.pallas_call(
        _stick_kernel,
        grid=(batch, N_PAIRS // PAIRS_PER_STEP, n_q),
        in_specs=[q_spec, kv_spec, kv_spec],
        out_specs=q_spec,
        out_shape=jax.ShapeDtypeStruct(q.shape, BF16),
        scratch_shapes=[pltpu.VMEM((TK, TK), BF16)] + [pair_scratch] * PAIRS_PER_STEP,
        compiler_params=pltpu.CompilerParams(
            dimension_semantics=("arbitrary", "arbitrary", "arbitrary"),
            vmem_limit_bytes=VMEM_LIMIT),
    )(q, k, v)


def _fox_kernel(stats_ref, q_ref, cq_ref, k_ref, ck_ref, v_ref, o_ref,
                q_scr, ones_scr, z_scr, p_scr, alpha_scr, m_scr, acc_scr, qnorm_scr, crow_scr):
    batch_idx = pl.program_id(0)
    pair = pl.program_id(1)
    t0 = pl.program_id(2) * TQ
    masks = _head_lane_masks()
    lane = lax.broadcasted_iota(jnp.int32, (1, LANES), 1)
    cq = cq_ref[...]
    gate_rows = []
    for h in range(HEADS_PER_STEP):
        g0 = (pair * HEADS_PER_STEP + h) * GATE_LANES
        gate_rows.append(jnp.where((lane >= g0) & (lane < g0 + GATE_LANES), cq, jnp.zeros_like(cq)))
    q_rows = _stack_heads(q_ref[...], masks)
    gates = jnp.concatenate(gate_rows, axis=0)
    q_scr[...] = jnp.concatenate([q_rows, gates], axis=1)
    lane_rows = lax.broadcasted_iota(jnp.int32, (HEADS_PER_STEP * TK, LANES), 1) // HEAD_DIM
    head_rows = lax.broadcasted_iota(jnp.int32, (HEADS_PER_STEP * TK, LANES), 0) // TK
    ones_scr[...] = (lane_rows == head_rows).astype(F32).astype(BF16)
    m_scr[...] = jnp.full_like(m_scr, NEG_INF)
    acc_scr[...] = jnp.zeros_like(acc_scr)

    q_f = q_rows.astype(F32)
    q_norm = jnp.sqrt(jnp.sum(q_f * q_f, axis=1, keepdims=True)) * NORM_MARGIN
    qnorm_scr[...] = jnp.broadcast_to(q_norm, qnorm_scr.shape)
    piece_lanes = (lane % GATE_LANES) < N_PIECES
    c_row = jnp.sum(jnp.where(piece_lanes, gates.astype(F32), 0.0), axis=1, keepdims=True)
    crow_scr[...] = jnp.broadcast_to(c_row, crow_scr.shape)

    def key_start(n):
        back = jnp.maximum(t0 - (n - 1) * TK, 0)
        return pl.multiple_of(jnp.where(n < 2, t0 + n * TK, back), TK)

    def visible(n):
        return _visible(t0, t0 + n * TK, TK, strict=False)

    def scores(n, slot):
        start = key_start(n)
        k_blk = jnp.concatenate([k_ref[pl.ds(start, TK), :], ck_ref[pl.ds(start, TK), :]], axis=1)
        z_scr[slot][...] = _dot_nt(q_scr[...], k_blk)

    def probs(slot, mask=None):
        z = z_scr[slot][...]
        if mask is not None:
            z = jnp.where(mask, z, NEG_INF)
        m = m_scr[...]
        m_new = jnp.maximum(m, jnp.max(z, axis=1, keepdims=True))
        m_scr[...] = m_new
        shift = jnp.concatenate([m_new] * (TK // LANES), axis=1)
        p_scr[slot][...] = _heads_to_lanes(jnp.exp(z - shift).astype(BF16))
        alpha = jnp.exp(m - m_new)
        alpha_scr[slot][...] = jnp.where(masks[0], alpha[:TQ], alpha[TQ:])

    def accumulate(n, slot):
        v_rows = jnp.concatenate(
            [_stack_heads(v_ref[pl.ds(key_start(n), TK), :], masks), ones_scr[...]], axis=1)
        alpha = alpha_scr[slot][...]
        acc_scr[...] = (jnp.concatenate([alpha, alpha], axis=1) * acc_scr[...]
                        + _dot(p_scr[slot][...], v_rows))

    def rest_is_negligible(n_done):
        last_key = key_start(n_done) - 1
        tile = jnp.maximum(last_key, 0) // ROW_TILE
        worst = None
        for h in range(HEADS_PER_STEP):
            head = pair * HEADS_PER_STEP + h
            c_floor = stats_ref[batch_idx, tile, head]
            k_norm = stats_ref[batch_idx, tile, N_MIX_HEADS + head]
            rows = slice(h * TQ, (h + 1) * TQ)
            gap = (qnorm_scr[rows, :] * k_norm + (crow_scr[rows, :] - c_floor)
                   - m_scr[rows, :] + FOX_NEGLIGIBLE)
            top = jnp.max(gap)
            worst = top if worst is None else jnp.maximum(worst, top)
        return worst <= 0.0

    assert TQ == 2 * TK
    n_blocks = TQ // TK + t0 // TK
    scores(0, 0)
    probs(0, visible(0)); scores(1, 1)
    accumulate(0, 0); probs(1, visible(1)); scores(2, 0)

    def more(carry):
        n, done = carry
        return jnp.logical_and(n < n_blocks - 2, jnp.logical_not(done))

    def trip(carry):
        n, _ = carry
        scores(n + 2, 1); accumulate(n, 1); probs(0)
        scores(n + 3, 0); accumulate(n + 1, 0); probs(1)
        return n + 2, rest_is_negligible(n + 2)

    n_end, _ = lax.while_loop(more, trip, (jnp.int32(1), jnp.bool_(False)))
    accumulate(n_end, 1)
    acc = acc_scr[...]
    o_ref[...] = (acc[:, :LANES] / acc[:, LANES:]).astype(o_ref.dtype)


def _fox_attention(q, cq, k, ck, v, stats, batch, seq_len):
    n_q = seq_len // TQ
    q_spec = pl.BlockSpec((TQ, LANES), lambda b, p, i: (b * n_q + i, p))
    cq_spec = pl.BlockSpec((TQ, LANES), lambda b, p, i: (b * n_q + i, 0))
    kv_spec = pl.BlockSpec((seq_len, LANES), lambda b, p, i: (b, p))
    ck_spec = pl.BlockSpec((seq_len, LANES), lambda b, p, i: (b, 0))
    return pl.pallas_call(
        _fox_kernel,
        grid=(batch, N_PAIRS, n_q),
        in_specs=[pl.BlockSpec(memory_space=pltpu.SMEM), q_spec, cq_spec, kv_spec, ck_spec, kv_spec],
        out_specs=q_spec,
        out_shape=jax.ShapeDtypeStruct(q.shape, BF16),
        scratch_shapes=[
            pltpu.VMEM((HEADS_PER_STEP * TQ, 2 * LANES), BF16),
            pltpu.VMEM((HEADS_PER_STEP * TK, LANES), BF16),
            _slot_pair((HEADS_PER_STEP * TQ, TK), F32),
            _slot_pair((TQ, HEADS_PER_STEP * TK), BF16),
            _slot_pair((TQ, LANES), F32),
            pltpu.VMEM((HEADS_PER_STEP * TQ, LANES), F32),
            pltpu.VMEM((TQ, 2 * LANES), F32),
            pltpu.VMEM((HEADS_PER_STEP * TQ, LANES), F32),
            pltpu.VMEM((HEADS_PER_STEP * TQ, LANES), F32),
        ],
        compiler_params=pltpu.CompilerParams(
            dimension_semantics=("arbitrary", "arbitrary", "arbitrary"),
            vmem_limit_bytes=VMEM_LIMIT),
    )(stats, q, cq, k, ck, v)


def _post_kernel(h_ref, mix_ref, qm_ref, mkv_ref, wo_ref, g2_ref, w1_ref, w2_ref, gf_ref, o_ref,
                 *, final_norm):
    qm = qm_ref[...]
    mem_k = mkv_ref[0, :, :MEM_WIDTH]
    mem_v = mkv_ref[0, :, MEM_WIDTH:]
    lane = lax.broadcasted_iota(jnp.int32, (1, MEM_WIDTH), 1)
    zero = jnp.zeros_like(qm)
    mem_out = jnp.zeros(qm.shape, F32)
    for hd in range(N_MEM_HEADS):
        head = (lane >= hd * HEAD_DIM) & (lane < (hd + 1) * HEAD_DIM)
        s = _dot_nt(jnp.where(head, qm, zero), mem_k)
        p = jnp.exp(s - jnp.max(s, axis=1, keepdims=True))
        p = p / jnp.sum(p, axis=1, keepdims=True)
        mem_out = jnp.where(head, _dot(p.astype(BF16), mem_v), mem_out)

    merged = jnp.concatenate([mix_ref[...], mem_out.astype(BF16)], axis=1)
    h1 = h_ref[...] + _dot(merged, wo_ref[...])

    xn = _rms_norm_rows(h1, g2_ref[...]).astype(BF16)
    mlp = jnp.zeros(h1.shape, F32)
    for c0 in range(0, D_FF, FF_TILE):
        a = jnp.maximum(_dot(xn, w1_ref[:, c0:c0 + FF_TILE]), 0.0)
        mlp = mlp + _dot((a * a).astype(BF16), w2_ref[c0:c0 + FF_TILE, :])
    h2 = h1 + mlp
    if final_norm:
        h2 = _rms_norm_rows(h2, gf_ref[...])
    o_ref[...] = h2


def _post(h_rows, mix, qm, mkv_l, wo, g2, w1, w2, gf, batch, final_norm):
    rows = h_rows.shape[0]
    tiles_per_seq = rows // batch // ROW_TILE
    n_mem = mkv_l.shape[1]
    row_spec = lambda wd: pl.BlockSpec((ROW_TILE, wd), lambda i: (i, 0))
    single = pl.Buffered(1)
    const = lambda shape: pl.BlockSpec(shape, lambda i: (0,) * len(shape), pipeline_mode=single)
    return pl.pallas_call(
        functools.partial(_post_kernel, final_norm=final_norm),
        grid=(rows // ROW_TILE,),
        in_specs=[
            row_spec(D_MODEL),
            row_spec(MIX_WIDTH),
            row_spec(MEM_WIDTH),
            pl.BlockSpec((1, n_mem, 2 * MEM_WIDTH), lambda i: (i // tiles_per_seq, 0, 0)),
            const(wo.shape),
            const((1, D_MODEL)),
            const(w1.shape),
            const(w2.shape),
            const((1, D_MODEL)),
        ],
        out_specs=row_spec(D_MODEL),
        out_shape=jax.ShapeDtypeStruct(h_rows.shape, F32),
        compiler_params=pltpu.CompilerParams(
            dimension_semantics=("arbitrary",), vmem_limit_bytes=VMEM_LIMIT),
    )(h_rows, mix, qm, mkv_l, wo, g2.reshape(1, D_MODEL), w1, w2, gf.reshape(1, D_MODEL))


def kernel(x, mem, norm1_g, w_in_a, w_in_b, w_mem_kv, mem_norm_g, w_o, norm2_g, w_mlp1, w_mlp2,
           kv_norm_g, w_kv_shared, b_f, final_norm_g):
    batch, seq_len, _ = x.shape
    n_mem = mem.shape[1]
    depth = norm1_g.shape[0]
    n_a = w_in_a.shape[0]
    assert seq_len % ROW_TILE == 0 and seq_len % TQ == 0 and TQ % TK == 0
    scale = HEAD_DIM ** -0.5

    q_scale_a = jnp.concatenate([jnp.full((MIX_WIDTH,), scale, F32), jnp.ones((2 * MIX_WIDTH,), F32),
                                 jnp.full((MEM_WIDTH,), scale, F32)])
    w_a = (w_in_a * q_scale_a).astype(BF16)
    w_b = (w_in_b * scale).astype(BF16)
    w_mkv = w_mem_kv.astype(BF16)
    w_out = w_o.astype(BF16)
    w1 = w_mlp1.astype(BF16)
    w2 = w_mlp2.astype(BF16)
    pad = LANES - N_MIX_HEADS
    w_kv = jnp.pad(w_kv_shared, ((0, 0), (0, pad))).astype(BF16)
    bf_pad = jnp.pad(b_f.astype(F32), (0, pad)).reshape(1, LANES)

    h = x.reshape(batch * seq_len, D_MODEL)
    mkv = _mem_kv(mem.reshape(batch * n_mem, D_MODEL), mem_norm_g, w_mkv)
    mkv = mkv.reshape(depth * batch, n_mem, 2 * MEM_WIDTH)

    k_sh = v_sh = cq = ck = stats = None
    for l in range(depth):
        if l == n_a:
            k_sh, v_sh, cq, ck, stats = _shared_kv(h, kv_norm_g, w_kv, bf_pad, seq_len)
            stats = stats.reshape(batch, seq_len // ROW_TILE, STATS_ROWS, LANES)[:, :, :N_STATS, :N_MIX_HEADS]
            stats = stats.reshape(batch, seq_len // ROW_TILE, N_STATS * N_MIX_HEADS)
        if l < n_a:
            q, k, v, qm = _norm_proj(h, norm1_g[l], w_a[l],
                                     (MIX_WIDTH, MIX_WIDTH, MIX_WIDTH, MEM_WIDTH),
                                     scales=(LOG2E, None, None, None))
            mix = _stick_attention(q, k, v, batch, seq_len)
        else:
            q, qm = _norm_proj(h, norm1_g[l], w_b[l - n_a], (MIX_WIDTH, MEM_WIDTH))
            mix = _fox_attention(q, cq, k_sh, ck, v_sh, stats, batch, seq_len)
        h = _post(h, mix, qm, mkv[l * batch:(l + 1) * batch], w_out[l], norm2_g[l], w1[l], w2[l],
                  final_norm_g, batch, final_norm=(l == depth - 1))
    return h.reshape(batch, seq_len, D_MODEL)
```

```python
import functools

import jax
import jax.numpy as jnp
import numpy as np
from jax import lax
from jax.experimental import pallas as pl
from jax.experimental.pallas import tpu as pltpu

D_MODEL = 1024
HEAD_DIM = 64
N_MIX_HEADS = 8
N_MEM_HEADS = 4
MIX_WIDTH = N_MIX_HEADS * HEAD_DIM
MEM_WIDTH = N_MEM_HEADS * HEAD_DIM
D_FF = 4 * D_MODEL
EPS = 1e-6
NEG_INF = -1e30
LOG2E = 1.4426950408889634
STICK_SPENT_LOG2 = 160.0
FOX_NEGLIGIBLE = 110.0
NORM_MARGIN = 1.01
STATS_ROWS = 8
N_STATS = 2

LANES = 128
HEADS_PER_STEP = LANES // HEAD_DIM
N_PAIRS = N_MIX_HEADS // HEADS_PER_STEP
PAIRS_PER_STEP = 2
GATE_LANES = 8
N_PIECES = 3

ROW_TILE = 512
TQ = 512
TK = 256
FF_TILE = 1024
VMEM_LIMIT = 56 * 1024 * 1024

BF16 = jnp.bfloat16
F32 = jnp.float32


def _dot(a, b):
    return jnp.dot(a, b, preferred_element_type=F32)


def _dot_nt(a, b):
    return lax.dot_general(a, b, (((1,), (1,)), ((), ())), preferred_element_type=F32)


def _rms_norm_rows(x, g):
    return x * lax.rsqrt(jnp.mean(x * x, axis=-1, keepdims=True) + EPS) * g


def _const_spec(shape):
    return pl.BlockSpec(shape, lambda *_: (0,) * len(shape))


def _slot_pair(shape, dtype):
    return (pltpu.VMEM(shape, dtype), pltpu.VMEM(shape, dtype))


def _norm_proj_kernel(x_ref, g_ref, w_ref, *o_refs, scales):
    xn = _rms_norm_rows(x_ref[...], g_ref[...]).astype(BF16)
    y = _dot(xn, w_ref[...])
    off = 0
    for o_ref, scale in zip(o_refs, scales):
        width = o_ref.shape[-1]
        part = y[:, off:off + width]
        o_ref[...] = (part if scale is None else part * scale).astype(o_ref.dtype)
        off += width


def _norm_proj(x, g, w, widths, scales=None):
    rows = x.shape[0]
    n = w.shape[1]
    assert sum(widths) == n and rows % ROW_TILE == 0
    scales = (None,) * len(widths) if scales is None else scales
    return pl.pallas_call(
        functools.partial(_norm_proj_kernel, scales=scales),
        grid=(rows // ROW_TILE,),
        in_specs=[
            pl.BlockSpec((ROW_TILE, D_MODEL), lambda i: (i, 0)),
            _const_spec((1, D_MODEL)),
            _const_spec((D_MODEL, n)),
        ],
        out_specs=[pl.BlockSpec((ROW_TILE, wd), lambda i: (i, 0)) for wd in widths],
        out_shape=[jax.ShapeDtypeStruct((rows, wd), BF16) for wd in widths],
        compiler_params=pltpu.CompilerParams(
            dimension_semantics=("arbitrary",), vmem_limit_bytes=VMEM_LIMIT),
    )(x, g.reshape(1, D_MODEL), w)


def _mem_kv_kernel(x_ref, g_ref, w_ref, o_ref):
    xn = _rms_norm_rows(x_ref[...], g_ref[0]).astype(BF16)
    o_ref[0] = _dot(xn, w_ref[0]).astype(o_ref.dtype)


def _mem_kv(mem_rows, gains, w):
    depth = w.shape[0]
    rows = mem_rows.shape[0]
    return pl.pallas_call(
        _mem_kv_kernel,
        grid=(depth,),
        in_specs=[
            _const_spec((rows, D_MODEL)),
            pl.BlockSpec((1, 1, D_MODEL), lambda l: (l, 0, 0)),
            pl.BlockSpec((1, D_MODEL, 2 * MEM_WIDTH), lambda l: (l, 0, 0)),
        ],
        out_specs=pl.BlockSpec((1, rows, 2 * MEM_WIDTH), lambda l: (l, 0, 0)),
        out_shape=jax.ShapeDtypeStruct((depth, rows, 2 * MEM_WIDTH), BF16),
        compiler_params=pltpu.CompilerParams(
            dimension_semantics=("arbitrary",), vmem_limit_bytes=VMEM_LIMIT),
    )(mem_rows, gains.reshape(depth, 1, D_MODEL), w)


def _split3(c):
    hi = c.astype(BF16)
    r1 = c - hi.astype(F32)
    mid = r1.astype(BF16)
    lo = (r1 - mid.astype(F32)).astype(BF16)
    return hi, mid, lo


def _shared_kv_kernel(x_ref, g_ref, w_ref, bf_ref, eq_ref, ek_ref, oneq_ref, onek_ref, ehead_ref,
                      k_ref, v_ref, cq_ref, ck_ref, stats_ref, carry_ref, knorm_ref, *, tiles_per_seq):
    i = pl.program_id(0)

    @pl.when(i % tiles_per_seq == 0)
    def _():
        carry_ref[...] = jnp.zeros_like(carry_ref)
        knorm_ref[...] = jnp.zeros_like(knorm_ref)

    xn = _rms_norm_rows(x_ref[...], g_ref[...]).astype(BF16)
    y = _dot(xn, w_ref[...])
    k_bf = y[:, :MIX_WIDTH].astype(BF16)
    k_ref[...] = k_bf
    v_ref[...] = y[:, MIX_WIDTH:2 * MIX_WIDTH].astype(BF16)

    k_f = k_bf.astype(F32)
    k_sq = _dot((k_f * k_f).astype(BF16), ehead_ref[...])
    knorm_ref[...] = jnp.maximum(knorm_ref[...], jnp.max(k_sq, axis=0, keepdims=True))

    f = y[:, 2 * MIX_WIDTH:] + bf_ref[...]
    log_f = jnp.minimum(f, 0.0) - jnp.log1p(jnp.exp(-jnp.abs(f)))
    lane = lax.broadcasted_iota(jnp.int32, log_f.shape, 1)
    log_f = jnp.where(lane < N_MIX_HEADS, log_f, 0.0)

    rows = log_f.shape[0]
    r = lax.broadcasted_iota(jnp.int32, (rows, rows), 0)
    c = lax.broadcasted_iota(jnp.int32, (rows, rows), 1)
    tri = (c <= r).astype(BF16)
    hi, mid, lo = _split3(log_f)
    cum = (_dot(tri, hi) + _dot(tri, mid)) + _dot(tri, lo) + carry_ref[...]
    carry_ref[...] = cum[rows - 1:rows, :]

    stats_ref[...] = jnp.concatenate(
        [cum[rows - 1:rows, :], jnp.sqrt(knorm_ref[...]) * NORM_MARGIN,
         jnp.zeros((STATS_ROWS - N_STATS, LANES), F32)], axis=0)

    pieces = jnp.concatenate(_split3(cum), axis=1)
    cq_ref[...] = (_dot(pieces, eq_ref[...]) + oneq_ref[...]).astype(BF16)
    ck_ref[...] = (onek_ref[...] - _dot(pieces, ek_ref[...])).astype(BF16)


def _gate_scatter_constants():
    eq = np.zeros((N_PIECES * LANES, LANES), np.float32)
    ek = np.zeros((N_PIECES * LANES, LANES), np.float32)
    oneq = np.zeros((1, LANES), np.float32)
    onek = np.zeros((1, LANES), np.float32)
    for h in range(N_MIX_HEADS):
        for p in range(N_PIECES):
            eq[p * LANES + h, GATE_LANES * h + p] = 1.0
            ek[p * LANES + h, GATE_LANES * h + N_PIECES + p] = 1.0
            oneq[0, GATE_LANES * h + N_PIECES + p] = 1.0
            onek[0, GATE_LANES * h + p] = 1.0
    return jnp.asarray(eq, BF16), jnp.asarray(ek, BF16), jnp.asarray(oneq), jnp.asarray(onek)


def _head_indicator():
    lane = np.arange(MIX_WIDTH)[:, None] // HEAD_DIM
    return jnp.asarray(lane == np.arange(LANES)[None, :], BF16)


def _shared_kv(h_rows, g, w_pad, bf_pad, seq_len):
    rows = h_rows.shape[0]
    n = w_pad.shape[1]
    eq, ek, oneq, onek = _gate_scatter_constants()
    ehead = _head_indicator()
    kern = functools.partial(_shared_kv_kernel, tiles_per_seq=seq_len // ROW_TILE)
    row_spec = lambda wd: pl.BlockSpec((ROW_TILE, wd), lambda i: (i, 0))
    return pl.pallas_call(
        kern,
        grid=(rows // ROW_TILE,),
        in_specs=[
            row_spec(D_MODEL),
            _const_spec((1, D_MODEL)),
            _const_spec((D_MODEL, n)),
            _const_spec((1, LANES)),
            _const_spec(eq.shape),
            _const_spec(ek.shape),
            _const_spec((1, LANES)),
            _const_spec((1, LANES)),
            _const_spec(ehead.shape),
        ],
        out_specs=[row_spec(MIX_WIDTH), row_spec(MIX_WIDTH), row_spec(LANES), row_spec(LANES),
                   pl.BlockSpec((STATS_ROWS, LANES), lambda i: (i, 0))],
        out_shape=[
            jax.ShapeDtypeStruct((rows, MIX_WIDTH), BF16),
            jax.ShapeDtypeStruct((rows, MIX_WIDTH), BF16),
            jax.ShapeDtypeStruct((rows, LANES), BF16),
            jax.ShapeDtypeStruct((rows, LANES), BF16),
            jax.ShapeDtypeStruct((rows // ROW_TILE * STATS_ROWS, LANES), F32),
        ],
        scratch_shapes=[pltpu.VMEM((1, LANES), F32), pltpu.VMEM((1, LANES), F32)],
        compiler_params=pltpu.CompilerParams(
            dimension_semantics=("arbitrary",), vmem_limit_bytes=VMEM_LIMIT),
    )(h_rows, g.reshape(1, D_MODEL), w_pad, bf_pad, eq, ek, oneq, onek, ehead)


def _head_lane_masks():
    lane = lax.broadcasted_iota(jnp.int32, (1, LANES), 1)
    return [(lane >= h * HEAD_DIM) & (lane < (h + 1) * HEAD_DIM) for h in range(HEADS_PER_STEP)]


def _stack_heads(x2, masks):
    zero = jnp.zeros_like(x2)
    return jnp.concatenate([jnp.where(m, x2, zero) for m in masks], axis=0)


def _heads_to_lanes(x):
    n = x.shape[0] // HEADS_PER_STEP
    return jnp.concatenate([x[h * n:(h + 1) * n] for h in range(HEADS_PER_STEP)], axis=1)


def _visible(t0, start, width, strict, rows=TQ):
    t_pos = t0 + (lax.broadcasted_iota(jnp.int32, (HEADS_PER_STEP * rows, width), 0) & (rows - 1))
    s_pos = start + lax.broadcasted_iota(jnp.int32, (HEADS_PER_STEP * rows, width), 1)
    return s_pos < t_pos if strict else s_pos <= t_pos


def _stick_pair(t0, lanes, masks, q_ref, k_ref, v_ref, o_ref, tri_scr,
                q_scr, z_scr, zl_scr, w_scr, later_scr, acc_scr):
    q_scr[...] = _stack_heads(q_ref[:, lanes], masks)
    later_scr[...] = jnp.zeros_like(later_scr)
    acc_scr[...] = jnp.zeros_like(acc_scr)

    def key_start(n):
        return pl.multiple_of(jnp.maximum(t0 + (TQ // TK - 1 - n) * TK, 0), TK)

    def visible(n):
        return _visible(t0, t0 + (TQ // TK - 1 - n) * TK, TK, strict=True)

    def scores(n, slot):
        z_scr[slot][...] = _dot_nt(q_scr[...], k_ref[pl.ds(key_start(n), TK), lanes])

    def weights(slot, mask=None):
        z = z_scr[slot][...]
        sp = jnp.maximum(z, 0.0) + jnp.log(1.0 + jnp.exp2(-jnp.abs(z))) * LOG2E
        if mask is not None:
            sp = jnp.where(mask, sp, 0.0)
        zl_scr[...] = z - later_scr[...]
        within = _dot(sp.astype(BF16), tri_scr[...])
        w = jnp.exp2(zl_scr[...] - within)
        if mask is not None:
            w = jnp.where(mask, w, 0.0)
        w_scr[slot][...] = _heads_to_lanes(w.astype(BF16))
        later_scr[...] += within[:, 0:1]

    def values(n, slot):
        acc_scr[...] += _dot(w_scr[slot][...], _stack_heads(v_ref[pl.ds(key_start(n), TK), lanes], masks))

    def latest_block():
        late = lambda x: jnp.concatenate(
            [x[h * TQ + TK:(h + 1) * TQ] for h in range(HEADS_PER_STEP)], axis=0)
        start = key_start(0)
        z = _dot_nt(late(q_scr[...]), k_ref[pl.ds(start, TK), lanes])
        mask = _visible(t0 + TK, start, TK, strict=True, rows=TQ - TK)
        sp = jnp.maximum(z, 0.0) + jnp.log(1.0 + jnp.exp2(-jnp.abs(z))) * LOG2E
        within = _dot(jnp.where(mask, sp, 0.0).astype(BF16), tri_scr[...])
        w = jnp.where(mask, jnp.exp2(z - within), 0.0)
        acc_scr[TK:TQ, :] += _dot(_heads_to_lanes(w.astype(BF16)),
                                  _stack_heads(v_ref[pl.ds(start, TK), lanes], masks))
        for h in range(HEADS_PER_STEP):
            later_scr[h * TQ + TK:(h + 1) * TQ, :] += within[h * (TQ - TK):(h + 1) * (TQ - TK), 0:1]

    assert TQ == 2 * TK
    n_blocks = TQ // TK + t0 // TK
    latest_block(); scores(1, 1)
    weights(1, visible(1)); scores(2, 0)
    yield

    def more(carry):
        n, spent = carry
        return jnp.logical_and(n < n_blocks - 2, jnp.logical_not(spent))

    def stick_spent():
        return jnp.min(later_scr[...]) >= STICK_SPENT_LOG2

    def trip(carry):
        n, _ = carry
        scores(n + 2, 1); values(n, 1); weights(0)
        spent_early = stick_spent()

        @pl.when(jnp.logical_not(spent_early))
        def _():
            scores(n + 3, 0); values(n + 1, 0); weights(1)

        return jnp.where(spent_early, n + 1, n + 2), stick_spent()

    n_end, _ = lax.while_loop(more, trip, (jnp.int32(1), jnp.bool_(False)))

    @pl.when(n_end % 2 == 1)
    def _():
        values(n_end, 1)

    @pl.when(n_end % 2 == 0)
    def _():
        values(n_end, 0)

    o_ref[:, lanes] = acc_scr[...].astype(o_ref.dtype)


def _stick_kernel(q_ref, k_ref, v_ref, o_ref, tri_scr, *pair_scratch):
    t0 = pl.program_id(2) * TQ
    masks = _head_lane_masks()
    r = lax.broadcasted_iota(jnp.int32, (TK, TK), 0)
    c = lax.broadcasted_iota(jnp.int32, (TK, TK), 1)
    tri_scr[...] = (r >= c).astype(BF16)
    pairs = [_stick_pair(t0, slice(p * LANES, (p + 1) * LANES), masks, q_ref, k_ref, v_ref, o_ref,
                         tri_scr, *pair_scratch[p]) for p in range(PAIRS_PER_STEP)]
    for program in pairs:
        next(program)
    for program in pairs:
        next(program, None)


def _stick_attention(q, k, v, batch, seq_len):
    n_q = seq_len // TQ
    width = PAIRS_PER_STEP * LANES
    q_spec = pl.BlockSpec((TQ, width), lambda b, g, i: (b * n_q + i, g))
    kv_spec = pl.BlockSpec((seq_len, width), lambda b, g, i: (b, g))
    pair_scratch = (
        pltpu.VMEM((HEADS_PER_STEP * TQ, LANES), BF16),
        _slot_pair((HEADS_PER_STEP * TQ, TK), F32),
        pltpu.VMEM((HEADS_PER_STEP * TQ, TK), F32),
        _slot_pair((TQ, HEADS_PER_STEP * TK), BF16),
        pltpu.VMEM((HEADS_PER_STEP * TQ, 1), F32),
        pltpu.VMEM((TQ, LANES), F32),
    )
    return pl.pallas_call(
        _stick_kernel,
        grid=(batch, N_PAIRS // PAIRS_PER_STEP, n_q),
        in_specs=[q_spec, kv_spec, kv_spec],
        out_specs=q_spec,
        out_shape=jax.ShapeDtypeStruct(q.shape, BF16),
        scratch_shapes=[pltpu.VMEM((TK, TK), BF16)] + [pair_scratch] * PAIRS_PER_STEP,
        compiler_params=pltpu.CompilerParams(
            dimension_semantics=("arbitrary", "arbitrary", "arbitrary"),
            vmem_limit_bytes=VMEM_LIMIT),
    )(q, k, v)


def _fox_pair(batch_idx, pair, t0, lanes, masks, stats_ref, q_ref, cq_ref, k_ref, ck_ref, v_ref, o_ref,
              ones_scr, q_scr, z_scr, p_scr, alpha_scr, m_scr, acc_scr, qnorm_scr, crow_scr):
    lane = lax.broadcasted_iota(jnp.int32, (1, LANES), 1)
    cq = cq_ref[...]
    gate_rows = []
    for h in range(HEADS_PER_STEP):
        g0 = (pair * HEADS_PER_STEP + h) * GATE_LANES
        gate_rows.append(jnp.where((lane >= g0) & (lane < g0 + GATE_LANES), cq, jnp.zeros_like(cq)))
    q_rows = _stack_heads(q_ref[:, lanes], masks)
    gates = jnp.concatenate(gate_rows, axis=0)
    q_scr[...] = jnp.concatenate([q_rows, gates], axis=1)
    m_scr[...] = jnp.full_like(m_scr, NEG_INF)
    acc_scr[...] = jnp.zeros_like(acc_scr)

    q_f = q_rows.astype(F32)
    q_norm = jnp.sqrt(jnp.sum(q_f * q_f, axis=1, keepdims=True)) * NORM_MARGIN
    qnorm_scr[...] = jnp.broadcast_to(q_norm, qnorm_scr.shape)
    piece_lanes = (lane % GATE_LANES) < N_PIECES
    c_row = jnp.sum(jnp.where(piece_lanes, gates.astype(F32), 0.0), axis=1, keepdims=True)
    crow_scr[...] = jnp.broadcast_to(c_row, crow_scr.shape)

    def key_start(n):
        back = jnp.maximum(t0 - (n - 1) * TK, 0)
        return pl.multiple_of(jnp.where(n < 2, t0 + n * TK, back), TK)

    def visible(n):
        return _visible(t0, t0 + n * TK, TK, strict=False)

    def scores(n, slot):
        start = key_start(n)
        k_blk = jnp.concatenate([k_ref[pl.ds(start, TK), lanes], ck_ref[pl.ds(start, TK), :]], axis=1)
        z_scr[slot][...] = _dot_nt(q_scr[...], k_blk)

    def probs(slot, mask=None):
        z = z_scr[slot][...]
        if mask is not None:
            z = jnp.where(mask, z, NEG_INF)
        m = m_scr[...]
        m_new = jnp.maximum(m, jnp.max(z, axis=1, keepdims=True))
        m_scr[...] = m_new
        shift = jnp.concatenate([m_new] * (TK // LANES), axis=1)
        p_scr[slot][...] = _heads_to_lanes(jnp.exp(z - shift).astype(BF16))
        alpha = jnp.exp(m - m_new)
        alpha_scr[slot][...] = jnp.where(masks[0], alpha[:TQ], alpha[TQ:])

    def accumulate(n, slot):
        v_rows = jnp.concatenate(
            [_stack_heads(v_ref[pl.ds(key_start(n), TK), lanes], masks), ones_scr[...]], axis=1)
        alpha = alpha_scr[slot][...]
        acc_scr[...] = (jnp.concatenate([alpha, alpha], axis=1) * acc_scr[...]
                        + _dot(p_scr[slot][...], v_rows))

    def rest_is_negligible(n_done):
        last_key = key_start(n_done) - 1
        tile = jnp.maximum(last_key, 0) // ROW_TILE
        worst = None
        for h in range(HEADS_PER_STEP):
            head = pair * HEADS_PER_STEP + h
            c_floor = stats_ref[batch_idx, tile, head]
            k_norm = stats_ref[batch_idx, tile, N_MIX_HEADS + head]
            rows = slice(h * TQ, (h + 1) * TQ)
            gap = (qnorm_scr[rows, :] * k_norm + (crow_scr[rows, :] - c_floor)
                   - m_scr[rows, :] + FOX_NEGLIGIBLE)
            top = jnp.max(gap)
            worst = top if worst is None else jnp.maximum(worst, top)
        return worst <= 0.0

    assert TQ == 2 * TK
    n_blocks = TQ // TK + t0 // TK
    scores(0, 0)
    probs(0, visible(0)); scores(1, 1)
    accumulate(0, 0); probs(1, visible(1)); scores(2, 0)
    yield

    def more(carry):
        n, done = carry
        return jnp.logical_and(n < n_blocks - 2, jnp.logical_not(done))

    def trip(carry):
        n, _ = carry
        scores(n + 2, 1); accumulate(n, 1); probs(0)
        scores(n + 3, 0); accumulate(n + 1, 0); probs(1)
        return n + 2, rest_is_negligible(n + 2)

    n_end, _ = lax.while_loop(more, trip, (jnp.int32(1), jnp.bool_(False)))
    accumulate(n_end, 1)
    acc = acc_scr[...]
    o_ref[:, lanes] = (acc[:, :LANES] / acc[:, LANES:]).astype(o_ref.dtype)


def _fox_kernel(stats_ref, q_ref, cq_ref, k_ref, ck_ref, v_ref, o_ref, ones_scr, *pair_scratch):
    batch_idx = pl.program_id(0)
    t0 = pl.program_id(2) * TQ
    masks = _head_lane_masks()
    lane_rows = lax.broadcasted_iota(jnp.int32, (HEADS_PER_STEP * TK, LANES), 1) // HEAD_DIM
    head_rows = lax.broadcasted_iota(jnp.int32, (HEADS_PER_STEP * TK, LANES), 0) // TK
    ones_scr[...] = (lane_rows == head_rows).astype(F32).astype(BF16)
    pairs = [_fox_pair(batch_idx, pl.program_id(1) * PAIRS_PER_STEP + p, t0,
                       slice(p * LANES, (p + 1) * LANES), masks,
                       stats_ref, q_ref, cq_ref, k_ref, ck_ref, v_ref, o_ref, ones_scr, *pair_scratch[p])
             for p in range(PAIRS_PER_STEP)]
    for program in pairs:
        next(program)
    for program in pairs:
        next(program, None)


def _fox_attention(q, cq, k, ck, v, stats, batch, seq_len):
    n_q = seq_len // TQ
    width = PAIRS_PER_STEP * LANES
    q_spec = pl.BlockSpec((TQ, width), lambda b, g, i: (b * n_q + i, g))
    cq_spec = pl.BlockSpec((TQ, LANES), lambda b, g, i: (b * n_q + i, 0))
    kv_spec = pl.BlockSpec((seq_len, width), lambda b, g, i: (b, g))
    ck_spec = pl.BlockSpec((seq_len, LANES), lambda b, g, i: (b, 0))
    pair_scratch = (
        pltpu.VMEM((HEADS_PER_STEP * TQ, 2 * LANES), BF16),
        _slot_pair((HEADS_PER_STEP * TQ, TK), F32),
        _slot_pair((TQ, HEADS_PER_STEP * TK), BF16),
        _slot_pair((TQ, LANES), F32),
        pltpu.VMEM((HEADS_PER_STEP * TQ, LANES), F32),
        pltpu.VMEM((TQ, 2 * LANES), F32),
        pltpu.VMEM((HEADS_PER_STEP * TQ, LANES), F32),
        pltpu.VMEM((HEADS_PER_STEP * TQ, LANES), F32),
    )
    return pl.pallas_call(
        _fox_kernel,
        grid=(batch, N_PAIRS // PAIRS_PER_STEP, n_q),
        in_specs=[pl.BlockSpec(memory_space=pltpu.SMEM), q_spec, cq_spec, kv_spec, ck_spec, kv_spec],
        out_specs=q_spec,
        out_shape=jax.ShapeDtypeStruct(q.shape, BF16),
        scratch_shapes=[pltpu.VMEM((HEADS_PER_STEP * TK, LANES), BF16)] + [pair_scratch] * PAIRS_PER_STEP,
        compiler_params=pltpu.CompilerParams(
            dimension_semantics=("arbitrary", "arbitrary", "arbitrary"),
            vmem_limit_bytes=VMEM_LIMIT),
    )(stats, q, cq, k, ck, v)


def _post_kernel(h_ref, mix_ref, qm_ref, mkv_ref, wo_ref, g2_ref, w1_ref, w2_ref, gf_ref, o_ref,
                 *, final_norm):
    qm = qm_ref[...]
    mem_k = mkv_ref[0, :, :MEM_WIDTH]
    mem_v = mkv_ref[0, :, MEM_WIDTH:]
    lane = lax.broadcasted_iota(jnp.int32, (1, MEM_WIDTH), 1)
    zero = jnp.zeros_like(qm)
    mem_out = jnp.zeros(qm.shape, F32)
    for hd in range(N_MEM_HEADS):
        head = (lane >= hd * HEAD_DIM) & (lane < (hd + 1) * HEAD_DIM)
        s = _dot_nt(jnp.where(head, qm, zero), mem_k)
        p = jnp.exp(s - jnp.max(s, axis=1, keepdims=True))
        p = p / jnp.sum(p, axis=1, keepdims=True)
        mem_out = jnp.where(head, _dot(p.astype(BF16), mem_v), mem_out)

    merged = jnp.concatenate([mix_ref[...], mem_out.astype(BF16)], axis=1)
    h1 = h_ref[...] + _dot(merged, wo_ref[...])

    xn = _rms_norm_rows(h1, g2_ref[...]).astype(BF16)
    mlp = jnp.zeros(h1.shape, F32)
    for c0 in range(0, D_FF, FF_TILE):
        a = jnp.maximum(_dot(xn, w1_ref[:, c0:c0 + FF_TILE]), 0.0)
        mlp = mlp + _dot((a * a).astype(BF16), w2_ref[c0:c0 + FF_TILE, :])
    h2 = h1 + mlp
    if final_norm:
        h2 = _rms_norm_rows(h2, gf_ref[...])
    o_ref[...] = h2


def _post(h_rows, mix, qm, mkv_l, wo, g2, w1, w2, gf, batch, final_norm):
    rows = h_rows.shape[0]
    tiles_per_seq = rows // batch // ROW_TILE
    n_mem = mkv_l.shape[1]
    row_spec = lambda wd: pl.BlockSpec((ROW_TILE, wd), lambda i: (i, 0))
    single = pl.Buffered(1)
    const = lambda shape: pl.BlockSpec(shape, lambda i: (0,) * len(shape), pipeline_mode=single)
    return pl.pallas_call(
        functools.partial(_post_kernel, final_norm=final_norm),
        grid=(rows // ROW_TILE,),
        in_specs=[
            row_spec(D_MODEL),
            row_spec(MIX_WIDTH),
            row_spec(MEM_WIDTH),
            pl.BlockSpec((1, n_mem, 2 * MEM_WIDTH), lambda i: (i // tiles_per_seq, 0, 0)),
            const(wo.shape),
            const((1, D_MODEL)),
            const(w1.shape),
            const(w2.shape),
            const((1, D_MODEL)),
        ],
        out_specs=row_spec(D_MODEL),
        out_shape=jax.ShapeDtypeStruct(h_rows.shape, F32),
        compiler_params=pltpu.CompilerParams(
            dimension_semantics=("arbitrary",), vmem_limit_bytes=VMEM_LIMIT),
    )(h_rows, mix, qm, mkv_l, wo, g2.reshape(1, D_MODEL), w1, w2, gf.reshape(1, D_MODEL))


def kernel(x, mem, norm1_g, w_in_a, w_in_b, w_mem_kv, mem_norm_g, w_o, norm2_g, w_mlp1, w_mlp2,
           kv_norm_g, w_kv_shared, b_f, final_norm_g):
    batch, seq_len, _ = x.shape
    n_mem = mem.shape[1]
    depth = norm1_g.shape[0]
    n_a = w_in_a.shape[0]
    assert seq_len % ROW_TILE == 0 and seq_len % TQ == 0 and TQ % TK == 0
    scale = HEAD_DIM ** -0.5

    q_scale_a = jnp.concatenate([jnp.full((MIX_WIDTH,), scale, F32), jnp.ones((2 * MIX_WIDTH,), F32),
                                 jnp.full((MEM_WIDTH,), scale, F32)])
    w_a = (w_in_a * q_scale_a).astype(BF16)
    w_b = (w_in_b * scale).astype(BF16)
    w_mkv = w_mem_kv.astype(BF16)
    w_out = w_o.astype(BF16)
    w1 = w_mlp1.astype(BF16)
    w2 = w_mlp2.astype(BF16)
    pad = LANES - N_MIX_HEADS
    w_kv = jnp.pad(w_kv_shared, ((0, 0), (0, pad))).astype(BF16)
    bf_pad = jnp.pad(b_f.astype(F32), (0, pad)).reshape(1, LANES)

    h = x.reshape(batch * seq_len, D_MODEL)
    mkv = _mem_kv(mem.reshape(batch * n_mem, D_MODEL), mem_norm_g, w_mkv)
    mkv = mkv.reshape(depth * batch, n_mem, 2 * MEM_WIDTH)

    k_sh = v_sh = cq = ck = stats = None
    for l in range(depth):
        if l == n_a:
            k_sh, v_sh, cq, ck, stats = _shared_kv(h, kv_norm_g, w_kv, bf_pad, seq_len)
            stats = stats.reshape(batch, seq_len // ROW_TILE, STATS_ROWS, LANES)[:, :, :N_STATS, :N_MIX_HEADS]
            stats = stats.reshape(batch, seq_len // ROW_TILE, N_STATS * N_MIX_HEADS)
        if l < n_a:
            q, k, v, qm = _norm_proj(h, norm1_g[l], w_a[l],
                                     (MIX_WIDTH, MIX_WIDTH, MIX_WIDTH, MEM_WIDTH),
                                     scales=(LOG2E, None, None, None))
            mix = _stick_attention(q, k, v, batch, seq_len)
        else:
            q, qm = _norm_proj(h, norm1_g[l], w_b[l - n_a], (MIX_WIDTH, MEM_WIDTH))
            mix = _fox_attention(q, cq, k_sh, ck, v_sh, stats, batch, seq_len)
        h = _post(h, mix, qm, mkv[l * batch:(l + 1) * batch], w_out[l], norm2_g[l], w1[l], w2[l],
                  final_norm_g, batch, final_norm=(l == depth - 1))
    return h.reshape(batch, seq_len, D_MODEL)
```

```python
import functools

import jax
import jax.numpy as jnp
import numpy as np
from jax import lax
from jax.experimental import pallas as pl
from jax.experimental.pallas import tpu as pltpu

D_MODEL = 1024
HEAD_DIM = 64
N_MIX_HEADS = 8
N_MEM_HEADS = 4
MIX_WIDTH = N_MIX_HEADS * HEAD_DIM
MEM_WIDTH = N_MEM_HEADS * HEAD_DIM
D_FF = 4 * D_MODEL
EPS = 1e-6
NEG_INF = -1e30
LOG2E = 1.4426950408889634
STICK_SPENT_LOG2 = 160.0
FOX_NEGLIGIBLE = 110.0
NORM_MARGIN = 1.01
STATS_ROWS = 8
N_STATS = 2

LANES = 128
HEADS_PER_STEP = LANES // HEAD_DIM
N_PAIRS = N_MIX_HEADS // HEADS_PER_STEP
PAIRS_PER_STEP = 2
GATE_LANES = 8
N_PIECES = 3

ROW_TILE = 512
TQ = 512
TK = 256
FF_TILE = 1024
VMEM_LIMIT = 56 * 1024 * 1024

BF16 = jnp.bfloat16
F32 = jnp.float32


def _dot(a, b):
    return jnp.dot(a, b, preferred_element_type=F32)


def _dot_nt(a, b):
    return lax.dot_general(a, b, (((1,), (1,)), ((), ())), preferred_element_type=F32)


def _rms_norm_rows(x, g):
    return x * lax.rsqrt(jnp.mean(x * x, axis=-1, keepdims=True) + EPS) * g


def _const_spec(shape):
    return pl.BlockSpec(shape, lambda *_: (0,) * len(shape))


def _slot_pair(shape, dtype):
    return (pltpu.VMEM(shape, dtype), pltpu.VMEM(shape, dtype))


def _norm_proj_kernel(x_ref, g_ref, w_ref, *o_refs, scales):
    xn = _rms_norm_rows(x_ref[...], g_ref[...]).astype(BF16)
    y = _dot(xn, w_ref[...])
    off = 0
    for o_ref, scale in zip(o_refs, scales):
        width = o_ref.shape[-1]
        part = y[:, off:off + width]
        o_ref[...] = (part if scale is None else part * scale).astype(o_ref.dtype)
        off += width


def _norm_proj(x, g, w, widths, scales=None):
    rows = x.shape[0]
    n = w.shape[1]
    assert sum(widths) == n and rows % ROW_TILE == 0
    scales = (None,) * len(widths) if scales is None else scales
    return pl.pallas_call(
        functools.partial(_norm_proj_kernel, scales=scales),
        grid=(rows // ROW_TILE,),
        in_specs=[
            pl.BlockSpec((ROW_TILE, D_MODEL), lambda i: (i, 0)),
            _const_spec((1, D_MODEL)),
            _const_spec((D_MODEL, n)),
        ],
        out_specs=[pl.BlockSpec((ROW_TILE, wd), lambda i: (i, 0)) for wd in widths],
        out_shape=[jax.ShapeDtypeStruct((rows, wd), BF16) for wd in widths],
        compiler_params=pltpu.CompilerParams(
            dimension_semantics=("arbitrary",), vmem_limit_bytes=VMEM_LIMIT),
    )(x, g.reshape(1, D_MODEL), w)


def _mem_kv_kernel(x_ref, g_ref, w_ref, o_ref):
    xn = _rms_norm_rows(x_ref[...], g_ref[0]).astype(BF16)
    o_ref[0] = _dot(xn, w_ref[0]).astype(o_ref.dtype)


def _mem_kv(mem_rows, gains, w):
    depth = w.shape[0]
    rows = mem_rows.shape[0]
    return pl.pallas_call(
        _mem_kv_kernel,
        grid=(depth,),
        in_specs=[
            _const_spec((rows, D_MODEL)),
            pl.BlockSpec((1, 1, D_MODEL), lambda l: (l, 0, 0)),
            pl.BlockSpec((1, D_MODEL, 2 * MEM_WIDTH), lambda l: (l, 0, 0)),
        ],
        out_specs=pl.BlockSpec((1, rows, 2 * MEM_WIDTH), lambda l: (l, 0, 0)),
        out_shape=jax.ShapeDtypeStruct((depth, rows, 2 * MEM_WIDTH), BF16),
        compiler_params=pltpu.CompilerParams(
            dimension_semantics=("arbitrary",), vmem_limit_bytes=VMEM_LIMIT),
    )(mem_rows, gains.reshape(depth, 1, D_MODEL), w)


def _split3(c):
    hi = c.astype(BF16)
    r1 = c - hi.astype(F32)
    mid = r1.astype(BF16)
    lo = (r1 - mid.astype(F32)).astype(BF16)
    return hi, mid, lo


def _shared_kv_kernel(x_ref, g_ref, w_ref, bf_ref, eq_ref, ek_ref, oneq_ref, onek_ref, ehead_ref,
                      k_ref, v_ref, cq_ref, ck_ref, stats_ref, carry_ref, knorm_ref, *, tiles_per_seq):
    i = pl.program_id(0)

    @pl.when(i % tiles_per_seq == 0)
    def _():
        carry_ref[...] = jnp.zeros_like(carry_ref)
        knorm_ref[...] = jnp.zeros_like(knorm_ref)

    xn = _rms_norm_rows(x_ref[...], g_ref[...]).astype(BF16)
    y = _dot(xn, w_ref[...])
    k_bf = y[:, :MIX_WIDTH].astype(BF16)
    k_ref[...] = k_bf
    v_ref[...] = y[:, MIX_WIDTH:2 * MIX_WIDTH].astype(BF16)

    k_f = k_bf.astype(F32)
    k_sq = _dot((k_f * k_f).astype(BF16), ehead_ref[...])
    knorm_ref[...] = jnp.maximum(knorm_ref[...], jnp.max(k_sq, axis=0, keepdims=True))

    f = y[:, 2 * MIX_WIDTH:] + bf_ref[...]
    log_f = jnp.minimum(f, 0.0) - jnp.log1p(jnp.exp(-jnp.abs(f)))
    lane = lax.broadcasted_iota(jnp.int32, log_f.shape, 1)
    log_f = jnp.where(lane < N_MIX_HEADS, log_f, 0.0)

    rows = log_f.shape[0]
    r = lax.broadcasted_iota(jnp.int32, (rows, rows), 0)
    c = lax.broadcasted_iota(jnp.int32, (rows, rows), 1)
    tri = (c <= r).astype(BF16)
    hi, mid, lo = _split3(log_f)
    cum = (_dot(tri, hi) + _dot(tri, mid)) + _dot(tri, lo) + carry_ref[...]
    carry_ref[...] = cum[rows - 1:rows, :]

    stats_ref[...] = jnp.concatenate(
        [cum[rows - 1:rows, :], jnp.sqrt(knorm_ref[...]) * NORM_MARGIN,
         jnp.zeros((STATS_ROWS - N_STATS, LANES), F32)], axis=0)

    pieces = jnp.concatenate(_split3(cum), axis=1)
    cq_ref[...] = (_dot(pieces, eq_ref[...]) + oneq_ref[...]).astype(BF16)
    ck_ref[...] = (onek_ref[...] - _dot(pieces, ek_ref[...])).astype(BF16)


def _gate_scatter_constants():
    eq = np.zeros((N_PIECES * LANES, LANES), np.float32)
    ek = np.zeros((N_PIECES * LANES, LANES), np.float32)
    oneq = np.zeros((1, LANES), np.float32)
    onek = np.zeros((1, LANES), np.float32)
    for h in range(N_MIX_HEADS):
        for p in range(N_PIECES):
            eq[p * LANES + h, GATE_LANES * h + p] = 1.0
            ek[p * LANES + h, GATE_LANES * h + N_PIECES + p] = 1.0
            oneq[0, GATE_LANES * h + N_PIECES + p] = 1.0
            onek[0, GATE_LANES * h + p] = 1.0
    return jnp.asarray(eq, BF16), jnp.asarray(ek, BF16), jnp.asarray(oneq), jnp.asarray(onek)


def _head_indicator():
    lane = np.arange(MIX_WIDTH)[:, None] // HEAD_DIM
    return jnp.asarray(lane == np.arange(LANES)[None, :], BF16)


def _shared_kv(h_rows, g, w_pad, bf_pad, seq_len):
    rows = h_rows.shape[0]
    n = w_pad.shape[1]
    eq, ek, oneq, onek = _gate_scatter_constants()
    ehead = _head_indicator()
    kern = functools.partial(_shared_kv_kernel, tiles_per_seq=seq_len // ROW_TILE)
    row_spec = lambda wd: pl.BlockSpec((ROW_TILE, wd), lambda i: (i, 0))
    return pl.pallas_call(
        kern,
        grid=(rows // ROW_TILE,),
        in_specs=[
            row_spec(D_MODEL),
            _const_spec((1, D_MODEL)),
            _const_spec((D_MODEL, n)),
            _const_spec((1, LANES)),
            _const_spec(eq.shape),
            _const_spec(ek.shape),
            _const_spec((1, LANES)),
            _const_spec((1, LANES)),
            _const_spec(ehead.shape),
        ],
        out_specs=[row_spec(MIX_WIDTH), row_spec(MIX_WIDTH), row_spec(LANES), row_spec(LANES),
                   pl.BlockSpec((STATS_ROWS, LANES), lambda i: (i, 0))],
        out_shape=[
            jax.ShapeDtypeStruct((rows, MIX_WIDTH), BF16),
            jax.ShapeDtypeStruct((rows, MIX_WIDTH), BF16),
            jax.ShapeDtypeStruct((rows, LANES), BF16),
            jax.ShapeDtypeStruct((rows, LANES), BF16),
            jax.ShapeDtypeStruct((rows // ROW_TILE * STATS_ROWS, LANES), F32),
        ],
        scratch_shapes=[pltpu.VMEM((1, LANES), F32), pltpu.VMEM((1, LANES), F32)],
        compiler_params=pltpu.CompilerParams(
            dimension_semantics=("arbitrary",), vmem_limit_bytes=VMEM_LIMIT),
    )(h_rows, g.reshape(1, D_MODEL), w_pad, bf_pad, eq, ek, oneq, onek, ehead)


def _head_lane_masks():
    lane = lax.broadcasted_iota(jnp.int32, (1, LANES), 1)
    return [(lane >= h * HEAD_DIM) & (lane < (h + 1) * HEAD_DIM) for h in range(HEADS_PER_STEP)]


def _stack_heads(x2, masks):
    zero = jnp.zeros_like(x2)
    return jnp.concatenate([jnp.where(m, x2, zero) for m in masks], axis=0)


def _heads_to_lanes(x):
    n = x.shape[0] // HEADS_PER_STEP
    return jnp.concatenate([x[h * n:(h + 1) * n] for h in range(HEADS_PER_STEP)], axis=1)


def _visible(t0, start, width, strict, rows=TQ):
    t_pos = t0 + (lax.broadcasted_iota(jnp.int32, (HEADS_PER_STEP * rows, width), 0) & (rows - 1))
    s_pos = start + lax.broadcasted_iota(jnp.int32, (HEADS_PER_STEP * rows, width), 1)
    return s_pos < t_pos if strict else s_pos <= t_pos


def _stick_pair(t0, lanes, masks, q_ref, k_ref, v_ref, o_ref, tri_scr,
                q_scr, z_scr, zl_scr, w_scr, later_scr, acc_scr):
    q_scr[...] = _stack_heads(q_ref[:, lanes], masks)
    later_scr[...] = jnp.zeros_like(later_scr)
    acc_scr[...] = jnp.zeros_like(acc_scr)

    def key_start(n):
        return pl.multiple_of(jnp.maximum(t0 + (TQ // TK - 1 - n) * TK, 0), TK)

    def visible(n):
        return _visible(t0, t0 + (TQ // TK - 1 - n) * TK, TK, strict=True)

    def scores(n, slot):
        z_scr[slot][...] = _dot_nt(q_scr[...], k_ref[pl.ds(key_start(n), TK), lanes])

    def weights(slot, mask=None):
        z = z_scr[slot][...]
        sp = jnp.maximum(z, 0.0) + jnp.log(1.0 + jnp.exp2(-jnp.abs(z))) * LOG2E
        if mask is not None:
            sp = jnp.where(mask, sp, 0.0)
        zl_scr[...] = z - later_scr[...]
        within = _dot(sp.astype(BF16), tri_scr[...])
        w = jnp.exp2(zl_scr[...] - within)
        if mask is not None:
            w = jnp.where(mask, w, 0.0)
        w_scr[slot][...] = _heads_to_lanes(w.astype(BF16))
        later_scr[...] += within[:, 0:1]

    def values(n, slot):
        acc_scr[...] += _dot(w_scr[slot][...], _stack_heads(v_ref[pl.ds(key_start(n), TK), lanes], masks))

    def latest_block():
        late = lambda x: jnp.concatenate(
            [x[h * TQ + TK:(h + 1) * TQ] for h in range(HEADS_PER_STEP)], axis=0)
        start = key_start(0)
        z = _dot_nt(late(q_scr[...]), k_ref[pl.ds(start, TK), lanes])
        mask = _visible(t0 + TK, start, TK, strict=True, rows=TQ - TK)
        sp = jnp.maximum(z, 0.0) + jnp.log(1.0 + jnp.exp2(-jnp.abs(z))) * LOG2E
        within = _dot(jnp.where(mask, sp, 0.0).astype(BF16), tri_scr[...])
        w = jnp.where(mask, jnp.exp2(z - within), 0.0)
        acc_scr[TK:TQ, :] += _dot(_heads_to_lanes(w.astype(BF16)),
                                  _stack_heads(v_ref[pl.ds(start, TK), lanes], masks))
        for h in range(HEADS_PER_STEP):
            later_scr[h * TQ + TK:(h + 1) * TQ, :] += within[h * (TQ - TK):(h + 1) * (TQ - TK), 0:1]

    assert TQ == 2 * TK
    n_blocks = TQ // TK + t0 // TK
    latest_block(); scores(1, 1)
    weights(1, visible(1)); scores(2, 0)
    yield

    def more(carry):
        n, spent = carry
        return jnp.logical_and(n < n_blocks - 2, jnp.logical_not(spent))

    def stick_spent():
        return jnp.min(later_scr[...]) >= STICK_SPENT_LOG2

    def trip(carry):
        n, _ = carry
        scores(n + 2, 1); values(n, 1); weights(0)
        spent_early = stick_spent()

        @pl.when(jnp.logical_not(spent_early))
        def _():
            scores(n + 3, 0); values(n + 1, 0); weights(1)

        return jnp.where(spent_early, n + 1, n + 2), stick_spent()

    n_end, _ = lax.while_loop(more, trip, (jnp.int32(1), jnp.bool_(False)))

    @pl.when(n_end % 2 == 1)
    def _():
        values(n_end, 1)

    @pl.when(n_end % 2 == 0)
    def _():
        values(n_end, 0)

    o_ref[:, lanes] = acc_scr[...].astype(o_ref.dtype)


def _stick_kernel(q_ref, k_ref, v_ref, o_ref, tri_scr, *pair_scratch):
    t0 = pl.program_id(2) * TQ
    masks = _head_lane_masks()
    r = lax.broadcasted_iota(jnp.int32, (TK, TK), 0)
    c = lax.broadcasted_iota(jnp.int32, (TK, TK), 1)
    tri_scr[...] = (r >= c).astype(BF16)
    pairs = [_stick_pair(t0, slice(p * LANES, (p + 1) * LANES), masks, q_ref, k_ref, v_ref, o_ref,
                         tri_scr, *pair_scratch[p]) for p in range(PAIRS_PER_STEP)]
    for program in pairs:
        next(program)
    for program in pairs:
        next(program, None)


def _stick_attention(q, k, v, batch, seq_len):
    n_q = seq_len // TQ
    width = PAIRS_PER_STEP * LANES
    q_spec = pl.BlockSpec((TQ, width), lambda b, g, i: (b * n_q + i, g))
    kv_spec = pl.BlockSpec((seq_len, width), lambda b, g, i: (b, g))
    pair_scratch = (
        pltpu.VMEM((HEADS_PER_STEP * TQ, LANES), BF16),
        _slot_pair((HEADS_PER_STEP * TQ, TK), F32),
        pltpu.VMEM((HEADS_PER_STEP * TQ, TK), F32),
        _slot_pair((TQ, HEADS_PER_STEP * TK), BF16),
        pltpu.VMEM((HEADS_PER_STEP * TQ, 1), F32),
        pltpu.VMEM((TQ, LANES), F32),
    )
    return pl.pallas_call(
        _stick_kernel,
        grid=(batch, N_PAIRS // PAIRS_PER_STEP, n_q),
        in_specs=[q_spec, kv_spec, kv_spec],
        out_specs=q_spec,
        out_shape=jax.ShapeDtypeStruct(q.shape, BF16),
        scratch_shapes=[pltpu.VMEM((TK, TK), BF16)] + [pair_scratch] * PAIRS_PER_STEP,
        compiler_params=pltpu.CompilerParams(
            dimension_semantics=("arbitrary", "arbitrary", "arbitrary"),
            vmem_limit_bytes=VMEM_LIMIT),
    )(q, k, v)


def _fox_pair(batch_idx, pair, t0, lanes, masks, stats_ref, q_ref, cq_ref, k_ref, ck_ref, v_ref, o_ref,
              ones_scr, q_scr, z_scr, p_scr, alpha_scr, m_scr, acc_scr, qnorm_scr, crow_scr):
    lane = lax.broadcasted_iota(jnp.int32, (1, LANES), 1)
    cq = cq_ref[...]
    gate_rows = []
    for h in range(HEADS_PER_STEP):
        g0 = (pair * HEADS_PER_STEP + h) * GATE_LANES
        gate_rows.append(jnp.where((lane >= g0) & (lane < g0 + GATE_LANES), cq, jnp.zeros_like(cq)))
    q_rows = _stack_heads(q_ref[:, lanes], masks)
    gates = jnp.concatenate(gate_rows, axis=0)
    q_scr[...] = jnp.concatenate([q_rows, gates], axis=1)
    m_scr[...] = jnp.full_like(m_scr, NEG_INF)
    acc_scr[...] = jnp.zeros_like(acc_scr)

    q_f = q_rows.astype(F32)
    q_norm = jnp.sqrt(jnp.sum(q_f * q_f, axis=1, keepdims=True)) * NORM_MARGIN
    qnorm_scr[...] = jnp.broadcast_to(q_norm, qnorm_scr.shape)
    piece_lanes = (lane % GATE_LANES) < N_PIECES
    c_row = jnp.sum(jnp.where(piece_lanes, gates.astype(F32), 0.0), axis=1, keepdims=True)
    crow_scr[...] = jnp.broadcast_to(c_row, crow_scr.shape)

    def key_start(n):
        back = jnp.maximum(t0 - (n - 1) * TK, 0)
        return pl.multiple_of(jnp.where(n < 2, t0 + n * TK, back), TK)

    def visible(n):
        return _visible(t0, t0 + n * TK, TK, strict=False)

    def scores(n, slot):
        start = key_start(n)
        k_blk = jnp.concatenate([k_ref[pl.ds(start, TK), lanes], ck_ref[pl.ds(start, TK), :]], axis=1)
        z_scr[slot][...] = _dot_nt(q_scr[...], k_blk)

    def probs(slot, mask=None):
        z = z_scr[slot][...]
        if mask is not None:
            z = jnp.where(mask, z, NEG_INF)
        m = m_scr[...]
        m_new = jnp.maximum(m, jnp.max(z, axis=1, keepdims=True))
        m_scr[...] = m_new
        shift = jnp.concatenate([m_new] * (TK // LANES), axis=1)
        p_scr[slot][...] = _heads_to_lanes(jnp.exp(z - shift).astype(BF16))
        alpha = jnp.exp(m - m_new)
        alpha_scr[slot][...] = jnp.where(masks[0], alpha[:TQ], alpha[TQ:])

    def accumulate(n, slot):
        v_rows = jnp.concatenate(
            [_stack_heads(v_ref[pl.ds(key_start(n), TK), lanes], masks), ones_scr[...]], axis=1)
        alpha = alpha_scr[slot][...]
        acc_scr[...] = (jnp.concatenate([alpha, alpha], axis=1) * acc_scr[...]
                        + _dot(p_scr[slot][...], v_rows))

    def rest_is_negligible(n_done):
        last_key = key_start(n_done) - 1
        tile = jnp.maximum(last_key, 0) // ROW_TILE
        worst = None
        for h in range(HEADS_PER_STEP):
            head = pair * HEADS_PER_STEP + h
            c_floor = stats_ref[batch_idx, tile, head]
            k_norm = stats_ref[batch_idx, tile, N_MIX_HEADS + head]
            rows = slice(h * TQ, (h + 1) * TQ)
            gap = (qnorm_scr[rows, :] * k_norm + (crow_scr[rows, :] - c_floor)
                   - m_scr[rows, :] + FOX_NEGLIGIBLE)
            top = jnp.max(gap)
            worst = top if worst is None else jnp.maximum(worst, top)
        return worst <= 0.0

    assert TQ == 2 * TK
    n_blocks = TQ // TK + t0 // TK
    scores(0, 0)
    probs(0, visible(0)); scores(1, 1)
    accumulate(0, 0); probs(1, visible(1)); scores(2, 0)
    yield

    def more(carry):
        n, done = carry
        return jnp.logical_and(n < n_blocks - 2, jnp.logical_not(done))

    def trip(carry):
        n, _ = carry
        scores(n + 2, 1); accumulate(n, 1); probs(0)
        scores(n + 3, 0); accumulate(n + 1, 0); probs(1)
        return n + 2, rest_is_negligible(n + 2)

    n_end, _ = lax.while_loop(more, trip, (jnp.int32(1), jnp.bool_(False)))
    accumulate(n_end, 1)
    acc = acc_scr[...]
    o_ref[:, lanes] = (acc[:, :LANES] / acc[:, LANES:]).astype(o_ref.dtype)


def _fox_kernel(stats_ref, q_ref, cq_ref, k_ref, ck_ref, v_ref, o_ref, ones_scr, *pair_scratch):
    batch_idx = pl.program_id(0)
    t0 = pl.program_id(2) * TQ
    masks = _head_lane_masks()
    lane_rows = lax.broadcasted_iota(jnp.int32, (HEADS_PER_STEP * TK, LANES), 1) // HEAD_DIM
    head_rows = lax.broadcasted_iota(jnp.int32, (HEADS_PER_STEP * TK, LANES), 0) // TK
    ones_scr[...] = (lane_rows == head_rows).astype(F32).astype(BF16)
    pairs = [_fox_pair(batch_idx, pl.program_id(1) * PAIRS_PER_STEP + p, t0,
                       slice(p * LANES, (p + 1) * LANES), masks,
                       stats_ref, q_ref, cq_ref, k_ref, ck_ref, v_ref, o_ref, ones_scr, *pair_scratch[p])
             for p in range(PAIRS_PER_STEP)]
    for program in pairs:
        next(program)
    for program in pairs:
        next(program, None)


def _fox_attention(q, cq, k, ck, v, stats, batch, seq_len):
    n_q = seq_len // TQ
    width = PAIRS_PER_STEP * LANES
    q_spec = pl.BlockSpec((TQ, width), lambda b, g, i: (b * n_q + i, g))
    cq_spec = pl.BlockSpec((TQ, LANES), lambda b, g, i: (b * n_q + i, 0))
    kv_spec = pl.BlockSpec((seq_len, width), lambda b, g, i: (b, g))
    ck_spec = pl.BlockSpec((seq_len, LANES), lambda b, g, i: (b, 0))
    pair_scratch = (
        pltpu.VMEM((HEADS_PER_STEP * TQ, 2 * LANES), BF16),
        _slot_pair((HEADS_PER_STEP * TQ, TK), F32),
        _slot_pair((TQ, HEADS_PER_STEP * TK), BF16),
        _slot_pair((TQ, LANES), F32),
        pltpu.VMEM((HEADS_PER_STEP * TQ, LANES), F32),
        pltpu.VMEM((TQ, 2 * LANES), F32),
        pltpu.VMEM((HEADS_PER_STEP * TQ, LANES), F32),
        pltpu.VMEM((HEADS_PER_STEP * TQ, LANES), F32),
    )
    return pl.pallas_call(
        _fox_kernel,
        grid=(batch, N_PAIRS // PAIRS_PER_STEP, n_q),
        in_specs=[pl.BlockSpec(memory_space=pltpu.SMEM), q_spec, cq_spec, kv_spec, ck_spec, kv_spec],
        out_specs=q_spec,
        out_shape=jax.ShapeDtypeStruct(q.shape, BF16),
        scratch_shapes=[pltpu.VMEM((HEADS_PER_STEP * TK, LANES), BF16)] + [pair_scratch] * PAIRS_PER_STEP,
        compiler_params=pltpu.CompilerParams(
            dimension_semantics=("arbitrary", "arbitrary", "arbitrary"),
            vmem_limit_bytes=VMEM_LIMIT),
    )(stats, q, cq, k, ck, v)


def _post_kernel(h_ref, mix_ref, qm_ref, mkv_ref, wo_ref, g2_ref, w1_ref, w2_ref, gf_ref, *rest,
                 final_norm, next_scales):
    if next_scales is None:
        (o_ref,) = rest
    else:
        gn_ref, wn_ref, o_ref, *p_refs = rest
    qm = qm_ref[...]
    mem_k = mkv_ref[0, :, :MEM_WIDTH]
    mem_v = mkv_ref[0, :, MEM_WIDTH:]
    lane = lax.broadcasted_iota(jnp.int32, (1, MEM_WIDTH), 1)
    zero = jnp.zeros_like(qm)
    mem_out = jnp.zeros(qm.shape, F32)
    for hd in range(N_MEM_HEADS):
        head = (lane >= hd * HEAD_DIM) & (lane < (hd + 1) * HEAD_DIM)
        s = _dot_nt(jnp.where(head, qm, zero), mem_k)
        p = jnp.exp(s - jnp.max(s, axis=1, keepdims=True))
        p = p / jnp.sum(p, axis=1, keepdims=True)
        mem_out = jnp.where(head, _dot(p.astype(BF16), mem_v), mem_out)

    merged = jnp.concatenate([mix_ref[...], mem_out.astype(BF16)], axis=1)
    h1 = h_ref[...] + _dot(merged, wo_ref[...])

    xn = _rms_norm_rows(h1, g2_ref[...]).astype(BF16)
    mlp = jnp.zeros(h1.shape, F32)
    for c0 in range(0, D_FF, FF_TILE):
        a = jnp.maximum(_dot(xn, w1_ref[:, c0:c0 + FF_TILE]), 0.0)
        mlp = mlp + _dot((a * a).astype(BF16), w2_ref[c0:c0 + FF_TILE, :])
    h2 = h1 + mlp
    if next_scales is not None:
        y = _dot(_rms_norm_rows(h2, gn_ref[...]).astype(BF16), wn_ref[...])
        off = 0
        for p_ref, scale in zip(p_refs, next_scales):
            width = p_ref.shape[-1]
            part = y[:, off:off + width]
            p_ref[...] = (part if scale is None else part * scale).astype(p_ref.dtype)
            off += width
    if final_norm:
        h2 = _rms_norm_rows(h2, gf_ref[...])
    o_ref[...] = h2


def _post(h_rows, mix, qm, mkv_l, wo, g2, w1, w2, gf, batch, final_norm, next_proj=None):
    rows = h_rows.shape[0]
    tiles_per_seq = rows // batch // ROW_TILE
    n_mem = mkv_l.shape[1]
    row_spec = lambda wd: pl.BlockSpec((ROW_TILE, wd), lambda i: (i, 0))
    single = pl.Buffered(1)
    const = lambda shape: pl.BlockSpec(shape, lambda i: (0,) * len(shape), pipeline_mode=single)
    in_specs = [
        row_spec(D_MODEL),
        row_spec(MIX_WIDTH),
        row_spec(MEM_WIDTH),
        pl.BlockSpec((1, n_mem, 2 * MEM_WIDTH), lambda i: (i // tiles_per_seq, 0, 0)),
        const(wo.shape),
        const((1, D_MODEL)),
        const(w1.shape),
        const(w2.shape),
        const((1, D_MODEL)),
    ]
    operands = [h_rows, mix, qm, mkv_l, wo, g2.reshape(1, D_MODEL), w1, w2, gf.reshape(1, D_MODEL)]
    out_specs = [row_spec(D_MODEL)]
    out_shape = [jax.ShapeDtypeStruct(h_rows.shape, F32)]
    next_scales = None
    if next_proj is not None:
        gn, wn, widths, next_scales = next_proj
        assert sum(widths) == wn.shape[1]
        in_specs += [const((1, D_MODEL)), const(wn.shape)]
        operands += [gn.reshape(1, D_MODEL), wn]
        out_specs += [row_spec(wd) for wd in widths]
        out_shape += [jax.ShapeDtypeStruct((rows, wd), BF16) for wd in widths]
    return pl.pallas_call(
        functools.partial(_post_kernel, final_norm=final_norm, next_scales=next_scales),
        grid=(rows // ROW_TILE,),
        in_specs=in_specs,
        out_specs=out_specs,
        out_shape=out_shape,
        compiler_params=pltpu.CompilerParams(
            dimension_semantics=("arbitrary",), vmem_limit_bytes=VMEM_LIMIT),
    )(*operands)


def kernel(x, mem, norm1_g, w_in_a, w_in_b, w_mem_kv, mem_norm_g, w_o, norm2_g, w_mlp1, w_mlp2,
           kv_norm_g, w_kv_shared, b_f, final_norm_g):
    batch, seq_len, _ = x.shape
    n_mem = mem.shape[1]
    depth = norm1_g.shape[0]
    n_a = w_in_a.shape[0]
    assert seq_len % ROW_TILE == 0 and seq_len % TQ == 0 and TQ % TK == 0
    scale = HEAD_DIM ** -0.5

    q_scale_a = jnp.concatenate([jnp.full((MIX_WIDTH,), scale, F32), jnp.ones((2 * MIX_WIDTH,), F32),
                                 jnp.full((MEM_WIDTH,), scale, F32)])
    w_a = (w_in_a * q_scale_a).astype(BF16)
    w_b = (w_in_b * scale).astype(BF16)
    w_mkv = w_mem_kv.astype(BF16)
    w_out = w_o.astype(BF16)
    w1 = w_mlp1.astype(BF16)
    w2 = w_mlp2.astype(BF16)
    pad = LANES - N_MIX_HEADS
    w_kv = jnp.pad(w_kv_shared, ((0, 0), (0, pad))).astype(BF16)
    bf_pad = jnp.pad(b_f.astype(F32), (0, pad)).reshape(1, LANES)

    h = x.reshape(batch * seq_len, D_MODEL)
    mkv = _mem_kv(mem.reshape(batch * n_mem, D_MODEL), mem_norm_g, w_mkv)
    mkv = mkv.reshape(depth * batch, n_mem, 2 * MEM_WIDTH)

    def in_proj(l):
        if l < n_a:
            return (norm1_g[l], w_a[l], (MIX_WIDTH, MIX_WIDTH, MIX_WIDTH, MEM_WIDTH),
                    (LOG2E, None, None, None))
        return norm1_g[l], w_b[l - n_a], (MIX_WIDTH, MEM_WIDTH), (None, None)

    k_sh = v_sh = cq = ck = stats = proj = None
    for l in range(depth):
        if l == n_a:
            k_sh, v_sh, cq, ck, stats = _shared_kv(h, kv_norm_g, w_kv, bf_pad, seq_len)
            stats = stats.reshape(batch, seq_len // ROW_TILE, STATS_ROWS, LANES)[:, :, :N_STATS, :N_MIX_HEADS]
            stats = stats.reshape(batch, seq_len // ROW_TILE, N_STATS * N_MIX_HEADS)
        if l == 0:
            proj = _norm_proj(h, *in_proj(0)[:3], scales=in_proj(0)[3])
        if l < n_a:
            q, k, v, qm = proj
            mix = _stick_attention(q, k, v, batch, seq_len)
        else:
            q, qm = proj
            mix = _fox_attention(q, cq, k_sh, ck, v_sh, stats, batch, seq_len)
        h, *proj = _post(h, mix, qm, mkv[l * batch:(l + 1) * batch], w_out[l], norm2_g[l], w1[l], w2[l],
                         final_norm_g, batch, final_norm=(l == depth - 1),
                         next_proj=in_proj(l + 1) if l + 1 < depth else None)
    return h.reshape(batch, seq_len, D_MODEL)
```
